```python
import math
import jax
import jax.numpy as jnp
from jax import lax
import numpy as np

D_MODEL = 4096
BATCH = 2
SEQ = 4096
DEPTH = 2
DEC_BATCH = 1
DEC_SEQ = 8192
PAST_LEN = 128

EPS = 1e-6
POOL_WINDOWS = (2, 4, 8, 16)
POOL_WIDTH = D_MODEL // 4
POOL_GROUP = POOL_WIDTH // len(POOL_WINDOWS)
MLA_HEADS = 16
Q_LORA = D_MODEL // 4
KV_LORA = 512
QK_NOPE = 128
QK_ROPE = 64
V_HEAD = 128
MLA_WIDTH = MLA_HEADS * V_HEAD
ROPE_BASE = 10000.0
Q_BLOCK = 128
SSM_WIDTH = D_MODEL // 4
SSM_GROUP = 16
SSM_GROUPS = SSM_WIDTH // SSM_GROUP
SSM_STATE = 64
DT_MIN = 1e-3
DT_MAX = 1e-1
N_BRANCH = 3
IN_SPLITS = (POOL_WIDTH,
             POOL_WIDTH + Q_LORA,
             POOL_WIDTH + Q_LORA + KV_LORA,
             POOL_WIDTH + Q_LORA + KV_LORA + QK_ROPE,
             POOL_WIDTH + Q_LORA + KV_LORA + QK_ROPE + SSM_WIDTH)
IN_COLS = IN_SPLITS[-1] + N_BRANCH * D_MODEL
PEER_HEADS = 8
N_KEYS = 128
N_EXPERTS = N_KEYS * N_KEYS
PEER_QDIM = 256
PEER_HALF = PEER_QDIM // 2
PEER_TOPK = 16
PEER_TOKEN_BLOCK = 128

kernel_name = 'hybrid_pool_mla_s5_peer_encoder'


def rmsnorm(x, g):
    xf = x.astype(jnp.float32)
    y = xf * lax.rsqrt(jnp.mean(xf * xf, axis=-1, keepdims=True) + EPS)
    return (y * g.astype(jnp.float32)).astype(x.dtype)


def pool_mixer(u, pool_w, pool_scale):
    bsz, seq, _ = u.shape
    uf = u.astype(jnp.float32)
    cs = jnp.concatenate([jnp.zeros_like(uf[:, :1]), jnp.cumsum(uf, axis=1)], axis=1)
    t = jnp.arange(seq)
    outs = []
    for gi, w in enumerate(POOL_WINDOWS):
        left = w // 2
        right = w - 1 - left
        lo = jnp.maximum(t - left, 0)
        hi = jnp.minimum(t + right, seq - 1) + 1
        sl = slice(gi * POOL_GROUP, (gi + 1) * POOL_GROUP)
        csg = cs[:, :, sl]
        total = jnp.take(csg, hi, axis=1) - jnp.take(csg, lo, axis=1)
        count = (hi - lo).astype(jnp.float32)[None, :, None]
        outs.append(total / count - uf[:, :, sl])
    p = jnp.stack(outs, axis=2)
    y = jnp.einsum('bsgc,gcd->bsgd', p, pool_w.astype(jnp.float32)).reshape(bsz, seq, POOL_WIDTH)
    return (y * pool_scale.astype(jnp.float32)).astype(u.dtype)


def rope_tables(seq):
    inv = 1.0 / (ROPE_BASE ** (jnp.arange(0, QK_ROPE, 2, dtype=jnp.float32) / QK_ROPE))
    ang = jnp.arange(seq, dtype=jnp.float32)[:, None] * inv[None, :]
    return jnp.cos(ang), jnp.sin(ang)


def apply_rope(x, cos, sin):
    half = x.shape[-1] // 2
    x1, x2 = x[..., :half], x[..., half:]
    c = cos[None, :, None, :]
    s = sin[None, :, None, :]
    return jnp.concatenate([x1 * c - x2 * s, x1 * s + x2 * c], axis=-1)


def mla(c_q, c_kv, k_pe, q_norm, w_uq, kv_norm, w_ukv):
    bsz, seq, _ = c_q.shape
    q = jnp.einsum('bsr,rhd->bshd', rmsnorm(c_q, q_norm), w_uq).astype(jnp.float32)
    kv = jnp.einsum('bsr,rhd->bshd', rmsnorm(c_kv, kv_norm), w_ukv).astype(jnp.float32)
    cos, sin = rope_tables(seq)
    scale = (QK_NOPE + QK_ROPE) ** -0.5
    q_nope = q[..., :QK_NOPE] * scale
    q_pe = apply_rope(q[..., QK_NOPE:], cos, sin) * scale
    k_nope = kv[..., :QK_NOPE]
    v = kv[..., QK_NOPE:]
    k_rot = apply_rope(k_pe.astype(jnp.float32)[:, :, None, :], cos, sin)[:, :, 0, :]
    nb = seq // Q_BLOCK

    def to_blocks(t):
        return jnp.swapaxes(t.reshape(bsz, nb, Q_BLOCK, MLA_HEADS, t.shape[-1]), 0, 1)

    def attend(blk):
        qn, qp = blk
        s = (jnp.einsum('bqhd,bkhd->bhqk', qn, k_nope)
             + jnp.einsum('bqhr,bkr->bhqk', qp, k_rot))
        p = jax.nn.softmax(s, axis=-1)
        return jnp.einsum('bhqk,bkhd->bqhd', p, v)

    o = lax.map(attend, (to_blocks(q_nope), to_blocks(q_pe)))
    return jnp.swapaxes(o, 0, 1).reshape(bsz, seq, MLA_WIDTH).astype(c_q.dtype)


def _ssm_combine(left, right):
    a_l, b_l = left
    a_r, b_r = right
    return a_r * a_l, a_r * b_l + b_r


def s5_bidirectional(u, a_re, a_im, log_dt, b_re, b_im, c_re, c_im, d_skip, w_glu):
    bsz, seq, _ = u.shape
    ug = u.astype(jnp.float32).reshape(bsz, seq, SSM_GROUPS, SSM_GROUP)
    lam = lax.complex(a_re.astype(jnp.float32), a_im.astype(jnp.float32))
    dt = jnp.exp(log_dt.astype(jnp.float32))[..., None]
    lam_bar = jnp.exp(lam * dt)
    b_mat = lax.complex(b_re.astype(jnp.float32), b_im.astype(jnp.float32))
    b_bar = ((lam_bar - 1.0) / lam)[..., None] * b_mat
    c_mat = lax.complex(c_re.astype(jnp.float32), c_im.astype(jnp.float32))
    uc = ug.astype(jnp.complex64)
    y = ug * d_skip.astype(jnp.float32).reshape(SSM_GROUPS, SSM_GROUP)
    for direction, reverse in ((0, False), (1, True)):
        bu = jnp.einsum('bsgp,gnp->bsgn', uc, b_bar[direction])
        a = jnp.broadcast_to(lam_bar[direction], bu.shape)
        _, h = lax.associative_scan(_ssm_combine, (a, bu), axis=1, reverse=reverse)
        y = y + jnp.einsum('bsgn,gpn->bsgp', h, c_mat[direction]).real
    y = jax.nn.gelu(y.reshape(bsz, seq, SSM_WIDTH))
    y = y * jax.nn.sigmoid(y @ w_glu.astype(jnp.float32))
    return y.astype(u.dtype)


def mixer_sublayer(x, norm_g, w_in, pool_w, pool_scale, q_norm, w_uq, kv_norm, w_ukv,
                   a_re, a_im, log_dt, b_re, b_im, c_re, c_im, d_skip, w_glu,
                   p_a, p_b, p_c, w_o):
    bsz, seq, _ = x.shape
    h = rmsnorm(x, norm_g)
    u_pool, c_q, c_kv, k_pe, u_ssm, gate_pre = jnp.split(h @ w_in, IN_SPLITS, axis=-1)
    y_a = pool_mixer(u_pool, pool_w, pool_scale) @ p_a
    y_b = mla(c_q, c_kv, k_pe, q_norm, w_uq, kv_norm, w_ukv) @ p_b
    y_c = s5_bidirectional(u_ssm, a_re, a_im, log_dt, b_re, b_im, c_re, c_im, d_skip, w_glu) @ p_c
    g = jax.nn.sigmoid(gate_pre.astype(jnp.float32)).reshape(bsz, seq, N_BRANCH, D_MODEL)
    merged = g[:, :, 0] * y_a + g[:, :, 1] * y_b + g[:, :, 2] * y_c
    return x + merged.astype(x.dtype) @ w_o


def peer_ffn(h, w_pq, sub_keys, u_tab, v_tab):
    bsz, seq, _ = h.shape
    n_tok = bsz * seq
    hf = h.reshape(n_tok, D_MODEL)
    q = (hf @ w_pq).astype(jnp.float32).reshape(n_tok, PEER_HEADS, 2, PEER_HALF)
    s = jnp.einsum('thcd,hckd->thck', q, sub_keys.astype(jnp.float32))
    top_s, top_i = lax.top_k(s, PEER_TOPK)
    cand = top_s[:, :, 0, :, None] + top_s[:, :, 1, None, :]
    best_s, best_pos = lax.top_k(cand.reshape(n_tok, PEER_HEADS, PEER_TOPK * PEER_TOPK), PEER_TOPK)
    i1 = jnp.take_along_axis(top_i[:, :, 0], best_pos // PEER_TOPK, axis=-1)
    i2 = jnp.take_along_axis(top_i[:, :, 1], best_pos % PEER_TOPK, axis=-1)
    expert = i1 * N_KEYS + i2
    gate = jax.nn.softmax(best_s, axis=-1)
    nb = n_tok // PEER_TOKEN_BLOCK

    def expert_block(args):
        xb, eb, gb = args
        u_sel = jnp.take(u_tab, eb, axis=0)
        act = jnp.einsum('td,thkd->thk', xb, u_sel)
        wgt = (gb * jax.nn.gelu(act.astype(jnp.float32))).astype(xb.dtype)
        v_sel = jnp.take(v_tab, eb, axis=0)
        return jnp.einsum('thk,thkd->td', wgt, v_sel)

    out = lax.map(expert_block, (hf.reshape(nb, PEER_TOKEN_BLOCK, D_MODEL),
                                 expert.reshape(nb, PEER_TOKEN_BLOCK, PEER_HEADS, PEER_TOPK),
                                 gate.reshape(nb, PEER_TOKEN_BLOCK, PEER_HEADS, PEER_TOPK)))
    return out.reshape(bsz, seq, D_MODEL)


def trunk(x, weights):
    (norm_mix, w_in, pool_w, pool_scale, q_norm, w_uq, kv_norm, w_ukv,
     ssm_a_re, ssm_a_im, ssm_log_dt, ssm_b_re, ssm_b_im, ssm_c_re, ssm_c_im, ssm_d, w_glu,
     p_a, p_b, p_c, w_o, norm_ffn, peer_wq, peer_keys, peer_u, peer_v, final_norm) = weights
    for l in range(DEPTH):
        x = mixer_sublayer(x, norm_mix[l], w_in[l], pool_w[l], pool_scale[l], q_norm[l], w_uq[l],
                           kv_norm[l], w_ukv[l], ssm_a_re[l], ssm_a_im[l], ssm_log_dt[l],
                           ssm_b_re[l], ssm_b_im[l], ssm_c_re[l], ssm_c_im[l], ssm_d[l], w_glu[l],
                           p_a[l], p_b[l], p_c[l], w_o[l])
        x = x + peer_ffn(rmsnorm(x, norm_ffn[l]), peer_wq[l], peer_keys[l], peer_u[l], peer_v[l])
    return rmsnorm(x, final_norm)


def setup_inputs(seed: int = 0) -> dict:
    key = jax.random.key(seed)
    k = jax.random.split(key, 29)
    L = DEPTH

    def nrm(i, shape, scale):
        return jax.random.normal(k[i], shape, jnp.float32) * scale

    def gain(i, shape):
        return 1.0 + 0.02 * jax.random.normal(k[i], shape, jnp.float32)

    ssm_shape = (L, 2, SSM_GROUPS, SSM_STATE)
    state_idx = jnp.arange(SSM_STATE, dtype=jnp.float32)
    return {
        'x_prompt': nrm(0, (BATCH, SEQ, D_MODEL), 1.0),
        'x_sample': nrm(1, (DEC_BATCH, DEC_SEQ, D_MODEL), 1.0),
        'norm_mix': gain(2, (L, D_MODEL)),
        'w_in': nrm(3, (L, D_MODEL, IN_COLS), D_MODEL ** -0.5),
        'pool_w': nrm(4, (L, len(POOL_WINDOWS), POOL_GROUP, POOL_GROUP), POOL_GROUP ** -0.5),
        'pool_scale': gain(5, (L, POOL_WIDTH)),
        'q_norm': gain(6, (L, Q_LORA)),
        'w_uq': nrm(7, (L, Q_LORA, MLA_HEADS, QK_NOPE + QK_ROPE), Q_LORA ** -0.5),
        'kv_norm': gain(8, (L, KV_LORA)),
        'w_ukv': nrm(9, (L, KV_LORA, MLA_HEADS, QK_NOPE + V_HEAD), KV_LORA ** -0.5),
        'ssm_a_re': -0.5 + nrm(10, ssm_shape, 0.01),
        'ssm_a_im': math.pi * state_idx + nrm(11, ssm_shape, 0.01),
        'ssm_log_dt': jax.random.uniform(k[12], (L, 2, SSM_GROUPS), jnp.float32,
                                         math.log(DT_MIN), math.log(DT_MAX)),
        'ssm_b_re': nrm(13, (L, 2, SSM_GROUPS, SSM_STATE, SSM_GROUP), (2 * SSM_GROUP) ** -0.5),
        'ssm_b_im': nrm(14, (L, 2, SSM_GROUPS, SSM_STATE, SSM_GROUP), (2 * SSM_GROUP) ** -0.5),
        'ssm_c_re': nrm(15, (L, 2, SSM_GROUPS, SSM_GROUP, SSM_STATE), (2 * SSM_STATE) ** -0.5),
        'ssm_c_im': nrm(16, (L, 2, SSM_GROUPS, SSM_GROUP, SSM_STATE), (2 * SSM_STATE) ** -0.5),
        'ssm_d': nrm(17, (L, SSM_WIDTH), 1.0),
        'w_glu': nrm(18, (L, SSM_WIDTH, SSM_WIDTH), SSM_WIDTH ** -0.5),
        'p_a': nrm(19, (L, POOL_WIDTH, D_MODEL), POOL_WIDTH ** -0.5),
        'p_b': nrm(20, (L, MLA_WIDTH, D_MODEL), MLA_WIDTH ** -0.5),
        'p_c': nrm(21, (L, SSM_WIDTH, D_MODEL), SSM_WIDTH ** -0.5),
        'w_o': nrm(22, (L, D_MODEL, D_MODEL), D_MODEL ** -0.5),
        'norm_ffn': gain(23, (L, D_MODEL)),
        'peer_wq': nrm(24, (L, D_MODEL, PEER_HEADS * PEER_QDIM), D_MODEL ** -0.5),
        'peer_keys': nrm(25, (L, PEER_HEADS, 2, N_KEYS, PEER_HALF), PEER_HALF ** -0.5),
        'peer_u': nrm(26, (L, N_EXPERTS, D_MODEL), D_MODEL ** -0.5),
        'peer_v': nrm(27, (L, N_EXPERTS, D_MODEL), PEER_HEADS ** -0.5),
        'final_norm': gain(28, (D_MODEL,)),
    }


def reference(x_prompt, x_sample, norm_mix, w_in, pool_w, pool_scale, q_norm, w_uq, kv_norm, w_ukv,
              ssm_a_re, ssm_a_im, ssm_log_dt, ssm_b_re, ssm_b_im, ssm_c_re, ssm_c_im, ssm_d, w_glu,
              p_a, p_b, p_c, w_o, norm_ffn, peer_wq, peer_keys, peer_u, peer_v, final_norm):
    weights = (norm_mix, w_in, pool_w, pool_scale, q_norm, w_uq, kv_norm, w_ukv,
               ssm_a_re, ssm_a_im, ssm_log_dt, ssm_b_re, ssm_b_im, ssm_c_re, ssm_c_im, ssm_d, w_glu,
               p_a, p_b, p_c, w_o, norm_ffn, peer_wq, peer_keys, peer_u, peer_v, final_norm)
    y_prompt = trunk(x_prompt, weights)
    y_sample = trunk(x_sample, weights)
    return (y_prompt, y_sample)
```

```python
import functools
import math

import jax
import jax.numpy as jnp
from jax import lax
from jax.experimental import pallas as pl
from jax.experimental.pallas import tpu as pltpu

F32 = jnp.float32
BF16 = jnp.bfloat16

EPS = 1e-6
POOL_WINDOWS = (2, 4, 8, 16)
QK_NOPE = 128
QK_ROPE = 64
V_HEAD = 128
ROPE_BASE = 10000.0
SSM_GROUP = 16
SSM_STATE = 64
PEER_TOPK = 16
N_KEYS = 128

LANES = 128
SUBLANES = 8
HEAD_PAD = 256
SSM_CHUNK = 16
VMEM_LIMIT = 56 * 1024 * 1024
POOL_HALO = 8


def _cparams(sem):
    return pltpu.CompilerParams(dimension_semantics=sem, vmem_limit_bytes=VMEM_LIMIT)


def _tile(pref, dim):
    t = min(pref, dim)
    assert dim % t == 0, (pref, dim)
    return t


def _gelu_tanh(x):
    cdf = 0.5 * (1.0 + jnp.tanh(math.sqrt(2.0 / math.pi) * (x + 0.044715 * (x * x * x))))
    return x * cdf


def _rms(x, g):
    y = x * lax.rsqrt(jnp.mean(x * x, axis=-1, keepdims=True) + EPS)
    return y * g


def _rmsnorm_body(x_ref, g_ref, o_ref):
    o_ref[...] = _rms(x_ref[...].astype(F32), g_ref[...]).astype(o_ref.dtype)


def rmsnorm(x, g, out_dtype, tm=512):
    t, d = x.shape
    tm = _tile(tm, t)
    return pl.pallas_call(
        _rmsnorm_body,
        grid=(t // tm,),
        in_specs=[pl.BlockSpec((tm, d), lambda i: (i, 0)),
                  pl.BlockSpec((1, d), lambda i: (0, 0))],
        out_specs=pl.BlockSpec((tm, d), lambda i: (i, 0)),
        out_shape=jax.ShapeDtypeStruct((t, d), out_dtype),
        compiler_params=_cparams(("parallel",)),
        name="rmsnorm",
    )(x, g.reshape(1, d).astype(F32))


def _mm_body(a_ref, b_ref, o_ref, *, act):
    acc = jnp.dot(a_ref[...], b_ref[...], preferred_element_type=F32)
    if act == "sigmoid":
        acc = jax.nn.sigmoid(acc)
    o_ref[...] = acc.astype(o_ref.dtype)


def _mm_res_body(a_ref, b_ref, r_ref, o_ref):
    acc = jnp.dot(a_ref[...], b_ref[...], preferred_element_type=F32)
    o_ref[...] = (r_ref[...] + acc).astype(o_ref.dtype)


def matmul(a, b, out_dtype, tm, tn, act=None, residual=None, name="matmul"):
    m, k = a.shape
    n = b.shape[1]
    tm = _tile(tm, m)
    tn = _tile(tn, n)
    in_specs = [pl.BlockSpec((tm, k), lambda i, j: (i, 0)),
                pl.BlockSpec((k, tn), lambda i, j: (0, j))]
    args = [a, b]
    if residual is None:
        body = functools.partial(_mm_body, act=act)
    else:
        body = _mm_res_body
        in_specs.append(pl.BlockSpec((tm, tn), lambda i, j: (i, j)))
        args.append(residual)
    return pl.pallas_call(
        body,
        grid=(m // tm, n // tn),
        in_specs=in_specs,
        out_specs=pl.BlockSpec((tm, tn), lambda i, j: (i, j)),
        out_shape=jax.ShapeDtypeStruct((m, n), out_dtype),
        compiler_params=_cparams(("parallel", "parallel")),
        name=name,
    )(*args)


def _seq_bounds(r0, seqs):
    s0 = jnp.int32(0)
    s1 = jnp.int32(0)
    for a, b in seqs:
        inside = jnp.logical_and(r0 >= a, r0 < b)
        s0 = jnp.where(inside, jnp.int32(a), s0)
        s1 = jnp.where(inside, jnp.int32(b), s1)
    return s0, s1


def _pool_body(prev_ref, cur_ref, next_ref, w_ref, sc_ref, o_ref, *, ts, seqs, group):
    r0 = pl.program_id(0) * ts
    s0, s1 = _seq_bounds(r0, seqs)
    cur = cur_ref[...].astype(F32)
    prev = jnp.where(r0 > s0, prev_ref[...].astype(F32), 0.0)
    nxt = jnp.where(r0 + ts < s1, next_ref[...].astype(F32), 0.0)
    ext = jnp.concatenate([prev, cur, nxt], axis=0).astype(BF16)
    te = ts + 2 * POOL_HALO
    d = lax.broadcasted_iota(jnp.int32, (ts, te), 1) - lax.broadcasted_iota(jnp.int32, (ts, te), 0)
    rows = r0 + lax.broadcasted_iota(jnp.int32, (ts, group), 0)
    for gi, w in enumerate(POOL_WINDOWS):
        left = w // 2
        right = w - 1 - left
        band = jnp.where(d >= POOL_HALO - left, jnp.where(d <= POOL_HALO + right, 1.0, 0.0), 0.0).astype(BF16)
        sl = slice(gi * group, (gi + 1) * group)
        tot = jnp.dot(band, ext[:, sl], preferred_element_type=F32)
        lo = jnp.maximum(rows - left, s0)
        hi = jnp.minimum(rows + right, s1 - 1)
        cnt = (hi - lo + 1).astype(F32)
        p = tot / cnt - cur[:, sl]
        y = jnp.dot(p.astype(BF16), w_ref[gi], preferred_element_type=F32) * sc_ref[:, sl]
        o_ref[:, sl] = y.astype(o_ref.dtype)


def pool_mixer(zs, pool_w, pool_scale, seqs, ts=512):
    t = zs.shape[0]
    ng, group, _ = pool_w.shape
    width = ng * group
    ts = _tile(ts, min(b - a for a, b in seqs))
    hb = ts // POOL_HALO
    nrow8 = t // POOL_HALO
    body = functools.partial(_pool_body, ts=ts, seqs=tuple(seqs), group=group)
    return pl.pallas_call(
        body,
        grid=(t // ts,),
        in_specs=[pl.BlockSpec((POOL_HALO, width), lambda i: (jnp.maximum(i * hb - 1, 0), 0)),
                  pl.BlockSpec((ts, width), lambda i: (i, 0)),
                  pl.BlockSpec((POOL_HALO, width), lambda i: (jnp.minimum((i + 1) * hb, nrow8 - 1), 0)),
                  pl.BlockSpec((ng, group, group), lambda i: (0, 0, 0)),
                  pl.BlockSpec((1, width), lambda i: (0, 0))],
        out_specs=pl.BlockSpec((ts, width), lambda i: (i, 0)),
        out_shape=jax.ShapeDtypeStruct((t, width), BF16),
        compiler_params=_cparams(("parallel",)),
        name="pool_mixer",
    )(zs, zs, zs, pool_w.astype(BF16), pool_scale.reshape(1, width).astype(F32))


def _rope_combine(x, cs):
    tmp = x * cs
    return tmp + pltpu.roll(tmp, LANES // 2, axis=1)


def _qproj_body(z_ref, g_ref, w_ref, cs_ref, o_ref, *, heads, scale):
    cn = _rms(z_ref[...].astype(F32), g_ref[...]).astype(BF16)
    acc = jnp.dot(cn, w_ref[...], preferred_element_type=F32)
    cs = cs_ref[...] * scale
    for h in range(heads):
        lo = acc[:, h * HEAD_PAD:h * HEAD_PAD + LANES] * scale
        hi = _rope_combine(acc[:, h * HEAD_PAD + LANES:(h + 1) * HEAD_PAD], cs)
        o_ref[:, h * HEAD_PAD:h * HEAD_PAD + LANES] = lo.astype(o_ref.dtype)
        o_ref[:, h * HEAD_PAD + LANES:(h + 1) * HEAD_PAD] = hi.astype(o_ref.dtype)


def q_proj(zs, col_block, q_norm, wq, cs, heads, tm=512):
    t = zs.shape[0]
    r = wq.shape[0]
    tm = _tile(tm, t)
    scale = (QK_NOPE + QK_ROPE) ** -0.5
    body = functools.partial(_qproj_body, heads=heads, scale=scale)
    return pl.pallas_call(
        body,
        grid=(t // tm,),
        in_specs=[pl.BlockSpec((tm, r), lambda i: (i, col_block)),
                  pl.BlockSpec((1, r), lambda i: (0, 0)),
                  pl.BlockSpec((r, heads * HEAD_PAD), lambda i: (0, 0)),
                  pl.BlockSpec((tm, LANES), lambda i: (i, 0))],
        out_specs=pl.BlockSpec((tm, heads * HEAD_PAD), lambda i: (i, 0)),
        out_shape=jax.ShapeDtypeStruct((t, heads * HEAD_PAD), BF16),
        compiler_params=_cparams(("parallel",)),
        name="q_proj",
    )(zs, q_norm.reshape(1, r).astype(F32), wq, cs)


def _kvproj_body(z_ref, g_ref, kpe_ref, wk_ref, wv_ref, cs_ref, k_ref, v_ref, *, heads):
    cn = _rms(z_ref[...].astype(F32), g_ref[...]).astype(BF16)
    kn = jnp.dot(cn, wk_ref[...], preferred_element_type=F32)
    v = jnp.dot(cn, wv_ref[...], preferred_element_type=F32)
    kr = _rope_combine(kpe_ref[...].astype(F32), cs_ref[...])
    lane = lax.broadcasted_iota(jnp.int32, kr.shape, 1)
    kr = jnp.where(lane < QK_ROPE, kr, 0.0).astype(k_ref.dtype)
    for h in range(heads):
        k_ref[:, h * HEAD_PAD:h * HEAD_PAD + LANES] = kn[:, h * LANES:(h + 1) * LANES].astype(k_ref.dtype)
        k_ref[:, h * HEAD_PAD + LANES:(h + 1) * HEAD_PAD] = kr
    v_ref[...] = v.astype(v_ref.dtype)


def kv_proj(zs, ckv_block, kpe_block, kv_norm, wk, wv, cs, heads, tm=512):
    t = zs.shape[0]
    r = wk.shape[0]
    tm = _tile(tm, t)
    body = functools.partial(_kvproj_body, heads=heads)
    return pl.pallas_call(
        body,
        grid=(t // tm,),
        in_specs=[pl.BlockSpec((tm, r), lambda i: (i, ckv_block)),
                  pl.BlockSpec((1, r), lambda i: (0, 0)),
                  pl.BlockSpec((tm, LANES), lambda i: (i, kpe_block)),
                  pl.BlockSpec((r, heads * QK_NOPE), lambda i: (0, 0)),
                  pl.BlockSpec((r, heads * V_HEAD), lambda i: (0, 0)),
                  pl.BlockSpec((tm, LANES), lambda i: (i, 0))],
        out_specs=[pl.BlockSpec((tm, heads * HEAD_PAD), lambda i: (i, 0)),
                   pl.BlockSpec((tm, heads * V_HEAD), lambda i: (i, 0))],
        out_shape=[jax.ShapeDtypeStruct((t, heads * HEAD_PAD), BF16),
                   jax.ShapeDtypeStruct((t, heads * V_HEAD), BF16)],
        compiler_params=_cparams(("parallel",)),
        name="kv_proj",
    )(zs, kv_norm.reshape(1, r).astype(F32), zs, wk, wv, cs)


def _attn_body(q_ref, k_ref, v_ref, o_ref, *, tk):
    q = q_ref[...]
    tq = q.shape[0]
    nk = k_ref.shape[0] // tk

    def step(j, carry):
        m, l, acc = carry
        off = pl.multiple_of(j * tk, tk)
        ks = k_ref[pl.ds(off, tk), :]
        vs = v_ref[pl.ds(off, tk), :]
        s = lax.dot_general(q, ks, (((1,), (1,)), ((), ())), preferred_element_type=F32)
        mn = jnp.maximum(m, jnp.max(s, axis=-1, keepdims=True))
        p = jnp.exp(s - mn)
        alpha = jnp.exp(m - mn)
        l = alpha * l + jnp.sum(p, axis=-1, keepdims=True)
        acc = alpha * acc + jnp.dot(p.astype(BF16), vs, preferred_element_type=F32)
        return mn, l, acc

    m0 = jnp.full((tq, 1), -jnp.inf, F32)
    l0 = jnp.zeros((tq, 1), F32)
    a0 = jnp.zeros((tq, V_HEAD), F32)
    _, l, acc = lax.fori_loop(0, nk, step, (m0, l0, a0))
    o_ref[...] = (acc / l).astype(o_ref.dtype)


def attention(qp, kp, v, row0, nseq, seq, heads, tq=512, tk=512):
    tq = _tile(tq, seq)
    tk = _tile(tk, seq)
    nq = seq // tq
    q0 = row0 // tq
    b0 = row0 // seq
    assert row0 % seq == 0
    body = functools.partial(_attn_body, tk=tk)
    return pl.pallas_call(
        body,
        grid=(nseq, heads, nq),
        in_specs=[pl.BlockSpec((tq, HEAD_PAD), lambda b, h, i: (q0 + b * nq + i, h)),
                  pl.BlockSpec((seq, HEAD_PAD), lambda b, h, i: (b0 + b, h)),
                  pl.BlockSpec((seq, V_HEAD), lambda b, h, i: (b0 + b, h))],
        out_specs=pl.BlockSpec((tq, V_HEAD), lambda b, h, i: (b * nq + i, h)),
        out_shape=jax.ShapeDtypeStruct((nseq * seq, heads * V_HEAD), BF16),
        compiler_params=_cparams(("parallel", "parallel", "parallel")),
        name="mla_attention",
    )(qp, kp, v)


def _s5_weights(a_re, a_im, log_dt, b_re, b_im, c_re, c_im, d_skip):
    L = SSM_CHUNK
    lam = lax.complex(a_re.astype(F32), a_im.astype(F32))
    dt = jnp.exp(log_dt.astype(F32))[..., None]
    z = lam * dt
    lam_bar = jnp.exp(z)
    ks = jnp.arange(L + 1, dtype=F32)
    pw = jnp.exp(z[..., None] * ks.astype(jnp.complex64))
    b_bar = ((lam_bar - 1.0) / lam)[..., None] * lax.complex(b_re.astype(F32), b_im.astype(F32))
    c_mat = lax.complex(c_re.astype(F32), c_im.astype(F32))
    g, n, p = b_bar.shape[1:]
    kern = jnp.einsum("dgpn,dgnk,dgnq->dgkpq", c_mat, pw[..., :L], b_bar).real
    li = jnp.arange(L)
    diff = li[None, :] - li[:, None]
    kf = jnp.take(kern[0], jnp.clip(diff, 0, L - 1), axis=1)
    kb = jnp.take(kern[1], jnp.clip(-diff, 0, L - 1), axis=1)
    dm = diff[None, :, :, None, None]
    eye = jnp.eye(p, dtype=F32) * d_skip.astype(F32).reshape(g, 1, 1, p, 1)
    m = jnp.where(dm > 0, kf, 0.0) + jnp.where(dm < 0, kb, 0.0) + jnp.where(dm == 0, kf + kb + eye, 0.0)
    m = m.transpose(0, 1, 4, 2, 3).reshape(g, L * p, L * p)

    cf = jnp.einsum("gnl,gnq->glqn", pw[0][..., L - 1 - li], b_bar[0])
    cb = jnp.einsum("gnl,gnq->glqn", pw[1][..., li], b_bar[1])
    cf = cf.reshape(g, L * p, n)
    cb = cb.reshape(g, L * p, n)
    bx = jnp.concatenate([cf.real, cf.imag, cf.imag, cf.real, cb.real, cb.imag, cb.imag, cb.real], axis=-1)

    wf = jnp.einsum("gpn,gnl->gnlp", c_mat[0], pw[0][..., li + 1]).reshape(g, n, L * p)
    wb = jnp.einsum("gpn,gnl->gnlp", c_mat[1], pw[1][..., L - li]).reshape(g, n, L * p)
    cyf = jnp.concatenate([wf.real, -wf.imag], axis=1)
    cyb = jnp.concatenate([wb.real, -wb.imag], axis=1)

    al = pw[..., L]

    def pack(x, y):
        return jnp.concatenate([x, y], axis=-1).reshape(1, g * 2 * n)

    a1f, a2f = pack(al[0].real, al[0].real), pack(-al[0].imag, al[0].imag)
    a1b, a2b = pack(al[1].real, al[1].real), pack(-al[1].imag, al[1].imag)
    return (m.astype(BF16), bx.astype(BF16), cyf.astype(BF16), cyb.astype(BF16),
            a1f.astype(F32), a2f.astype(F32), a1b.astype(F32), a2b.astype(F32))


def _s5x_body(u_ref, bx_ref, xf_ref, xfs_ref, xb_ref, xbs_ref):
    x = jnp.dot(u_ref[0], bx_ref[0], preferred_element_type=F32)
    xf_ref[...] = x[:, 0 * LANES:1 * LANES]
    xfs_ref[...] = x[:, 1 * LANES:2 * LANES]
    xb_ref[...] = x[:, 2 * LANES:3 * LANES]
    xbs_ref[...] = x[:, 3 * LANES:4 * LANES]


def _s5scan_body(xf_ref, xfs_ref, xb_ref, xbs_ref, a1f_ref, a2f_ref, a1b_ref, a2b_ref, hf_ref, hb_ref, *,
                 seq_chunks):
    a1f, a2f, a1b, a2b = a1f_ref[...], a2f_ref[...], a1b_ref[...], a2b_ref[...]
    zero = jnp.zeros_like(a1f)
    for c0, n in seq_chunks:

        def step(i, carry, c0=c0, n=n):
            hf, hfs, hb, hbs = carry
            rf = c0 + i
            rb = c0 + n - 1 - i
            hf_ref[pl.ds(rf, 1), :] = hf
            hb_ref[pl.ds(rb, 1), :] = hb
            xf = xf_ref[pl.ds(rf, 1), :]
            xfs = xfs_ref[pl.ds(rf, 1), :]
            xb = xb_ref[pl.ds(rb, 1), :]
            xbs = xbs_ref[pl.ds(rb, 1), :]
            nhf = a1f * hf + a2f * hfs + xf
            nhfs = a1f * hfs - a2f * hf + xfs
            nhb = a1b * hb + a2b * hbs + xb
            nhbs = a1b * hbs - a2b * hb + xbs
            return nhf, nhfs, nhb, nhbs

        lax.fori_loop(0, n, step, (zero, zero, zero, zero))


def _s5y_body(u_ref, m_ref, hf_ref, hb_ref, cyf_ref, cyb_ref, y_ref):
    y = jnp.dot(u_ref[0], m_ref[0], preferred_element_type=F32)
    y = y + jnp.dot(hf_ref[...].astype(BF16), cyf_ref[0], preferred_element_type=F32)
    y = y + jnp.dot(hb_ref[...].astype(BF16), cyb_ref[0], preferred_element_type=F32)
    y_ref[0] = y


def s5_mix(u, weights, seqs):
    m, bx, cyf, cyb, a1f, a2f, a1b, a2b = weights
    t = u.shape[0]
    g = m.shape[0]
    L, p = SSM_CHUNK, SSM_GROUP
    lp = L * p
    c = t // L
    ug = u.reshape(c, L, g, p).transpose(2, 0, 1, 3).reshape(g, c, lp).astype(BF16)
    xs = pl.pallas_call(
        _s5x_body,
        grid=(g,),
        in_specs=[pl.BlockSpec((1, c, lp), lambda i: (i, 0, 0)),
                  pl.BlockSpec((1, lp, 4 * LANES), lambda i: (i, 0, 0))],
        out_specs=[pl.BlockSpec((c, LANES), lambda i: (0, i))] * 4,
        out_shape=[jax.ShapeDtypeStruct((c, g * LANES), F32)] * 4,
        compiler_params=_cparams(("parallel",)),
        name="s5_chunk_states",
    )(ug, bx)
    seq_chunks = tuple((a // L, (b - a) // L) for a, b in seqs)
    tc = _tile(512, g * LANES)
    xspec = pl.BlockSpec((c, tc), lambda i: (0, i))
    aspec = pl.BlockSpec((1, tc), lambda i: (0, i))
    hf, hb = pl.pallas_call(
        functools.partial(_s5scan_body, seq_chunks=seq_chunks),
        grid=(g * LANES // tc,),
        in_specs=[xspec] * 4 + [aspec] * 4,
        out_specs=[xspec] * 2,
        out_shape=[jax.ShapeDtypeStruct((c, g * LANES), F32)] * 2,
        compiler_params=_cparams(("parallel",)),
        name="s5_state_scan",
    )(*xs, a1f, a2f, a1b, a2b)
    yg = pl.pallas_call(
        _s5y_body,
        grid=(g,),
        in_specs=[pl.BlockSpec((1, c, lp), lambda i: (i, 0, 0)),
                  pl.BlockSpec((1, lp, lp), lambda i: (i, 0, 0)),
                  pl.BlockSpec((c, LANES), lambda i: (0, i)),
                  pl.BlockSpec((c, LANES), lambda i: (0, i)),
                  pl.BlockSpec((1, LANES, lp), lambda i: (i, 0, 0)),
                  pl.BlockSpec((1, LANES, lp), lambda i: (i, 0, 0))],
        out_specs=pl.BlockSpec((1, c, lp), lambda i: (i, 0, 0)),
        out_shape=jax.ShapeDtypeStruct((g, c, lp), F32),
        compiler_params=_cparams(("parallel",)),
        name="s5_outputs",
    )(ug, m, hf, hb, cyf, cyb)
    return yg.reshape(g, c, L, p).transpose(1, 2, 0, 3).reshape(t, g * p)


def _glu_body(y_ref, w_ref, o_ref):
    gl = _gelu_tanh(y_ref[...])
    zz = jnp.dot(gl.astype(BF16), w_ref[...], preferred_element_type=F32)
    o_ref[...] = (gl * jax.nn.sigmoid(zz)).astype(o_ref.dtype)


def glu(y, w_glu, tm=512):
    t, w = y.shape
    tm = _tile(tm, t)
    return pl.pallas_call(
        _glu_body,
        grid=(t // tm,),
        in_specs=[pl.BlockSpec((tm, w), lambda i: (i, 0)),
                  pl.BlockSpec((w, w), lambda i: (0, 0))],
        out_specs=pl.BlockSpec((tm, w), lambda i: (i, 0)),
        out_shape=jax.ShapeDtypeStruct((t, w), BF16),
        compiler_params=_cparams(("parallel",)),
        name="s5_glu",
    )(y, w_glu)


def _merge_body(ya_ref, yb_ref, yc_ref, pa_ref, pb_ref, pc_ref, ga_ref, gb_ref, gc_ref, o_ref):
    a = jnp.dot(ya_ref[...], pa_ref[...], preferred_element_type=F32)
    b = jnp.dot(yb_ref[...], pb_ref[...], preferred_element_type=F32)
    c = jnp.dot(yc_ref[...], pc_ref[...], preferred_element_type=F32)
    o = ga_ref[...].astype(F32) * a + gb_ref[...].astype(F32) * b + gc_ref[...].astype(F32) * c
    o_ref[...] = o.astype(o_ref.dtype)


def gated_merge(ya, yb, yc, pa, pb, pc, gates, tm=512, tn=1024):
    t = ya.shape[0]
    d = pa.shape[1]
    tm = _tile(tm, t)
    tn = _tile(tn, d)
    nj = d // tn

    def aspec(x):
        return pl.BlockSpec((tm, x.shape[1]), lambda i, j: (i, 0))

    def wspec(x):
        return pl.BlockSpec((x.shape[0], tn), lambda i, j: (0, j))

    gspecs = [pl.BlockSpec((tm, tn), functools.partial(lambda i, j, k: (i, k * nj + j), k=k)) for k in range(3)]
    return pl.pallas_call(
        _merge_body,
        grid=(t // tm, nj),
        in_specs=[aspec(ya), aspec(yb), aspec(yc), wspec(pa), wspec(pb), wspec(pc)] + gspecs,
        out_specs=pl.BlockSpec((tm, tn), lambda i, j: (i, j)),
        out_shape=jax.ShapeDtypeStruct((t, d), BF16),
        compiler_params=_cparams(("parallel", "parallel")),
        name="gated_merge",
    )(ya, yb, yc, pa, pb, pc, gates, gates, gates)


def _oddeven_merge(lo, hi, r):
    step = r * 2
    if step < hi - lo:
        yield from _oddeven_merge(lo, hi, step)
        yield from _oddeven_merge(lo + r, hi, step)
        yield from [(i, i + r) for i in range(lo + r, hi - r, step)]
    else:
        yield (lo, lo + r)


def _oddeven_merge_sort(lo, hi):
    if hi - lo >= 1:
        mid = lo + (hi - lo) // 2
        yield from _oddeven_merge_sort(lo, mid)
        yield from _oddeven_merge_sort(mid + 1, hi)
        yield from _oddeven_merge(lo, hi, 1)


_SORT16 = tuple(_oddeven_merge_sort(0, PEER_TOPK - 1))
_BITONIC16 = tuple((i, i + d) for d in (8, 4, 2, 1) for i in range(PEER_TOPK) if not i & d)


def _compare_exchange(v, pairs):
    for i, j in pairs:
        hi = jnp.maximum(v[i], v[j])
        lo = jnp.minimum(v[i], v[j])
        v[i], v[j] = hi, lo


def _top16_desc(v):
    v = list(v)
    _compare_exchange(v, _SORT16)
    for shift in (4, 2, 1):
        r = [pltpu.roll(x, shift, axis=0) for x in v]
        v = [jnp.maximum(v[i], r[PEER_TOPK - 1 - i]) for i in range(PEER_TOPK)]
        _compare_exchange(v, _BITONIC16)
    return v


def _peer_select_body(q_ref, keys_ref, s1_ref, s2_ref, e2_ref, tau_ref, m1_ref, *, heads):
    tt = q_ref.shape[0]
    sub = lax.broadcasted_iota(jnp.int32, (SUBLANES, tt), 0)
    ninf = jnp.full((SUBLANES, tt), -jnp.inf, F32)

    def by_sublane(lst):
        out = lst[SUBLANES - 1]
        for j in range(SUBLANES - 2, -1, -1):
            out = jnp.where(sub == j, lst[j], out)
        return out

    for h in range(heads):
        scores, tops = [], []
        for c in range(2):
            blk = 2 * h + c
            qb = q_ref[:, blk * N_KEYS:(blk + 1) * N_KEYS]
            kb = keys_ref[blk]
            q_hi, k_hi = qb.astype(BF16), kb.astype(BF16)
            q_lo = (qb - q_hi.astype(F32)).astype(BF16)
            k_lo = (kb - k_hi.astype(F32)).astype(BF16)
            nt = (((1,), (1,)), ((), ()))
            st = (lax.dot_general(k_hi, q_hi, nt, preferred_element_type=F32)
                  + lax.dot_general(k_hi, q_lo, nt, preferred_element_type=F32)
                  + lax.dot_general(k_lo, q_hi, nt, preferred_element_type=F32))
            scores.append(st)
            tops.append(_top16_desc([st[SUBLANES * a:SUBLANES * (a + 1), :] for a in range(N_KEYS // SUBLANES)]))
        t1, t2 = tops
        t2a, t2b, t1b = by_sublane(t2[:8]), by_sublane(t2[8:]), by_sublane(t1[8:])
        cand = [t1[0] + t2a, t1[0] + t2b, t1[1] + t2a,
                jnp.where(sub < 5, t1[2] + t2a, ninf), jnp.where(sub < 4, t1[3] + t2a, ninf),
                jnp.where(sub < 3, t1[4] + t2a, ninf), jnp.where(sub < 2, t1[5] + t2a, ninf),
                jnp.where(sub < 2, t1[6] + t2a, ninf), jnp.where(sub < 2, t1[7] + t2a, ninf),
                t1b + t2[0]] + [ninf] * 6
        best = _top16_desc(cand)
        z = jnp.exp(best[0] - best[0])
        for b in best[1:]:
            z = z + jnp.exp(b - best[0])
        s1_ref[h] = scores[0]
        s2_ref[h] = scores[1]
        e2_ref[h] = jnp.exp(scores[1] - t2[0][0:1, :]) / z[0:1, :]
        tau_ref[h:h + 1, :] = best[PEER_TOPK - 1][0:1, :]
        m1_ref[h:h + 1, :] = t1[0][0:1, :]


def peer_select(q, keys, heads, tt=256):
    t = q.shape[0]
    tt = _tile(tt, t)
    big = jax.ShapeDtypeStruct((heads, N_KEYS, t), F32)
    small = jax.ShapeDtypeStruct((heads, t), F32)
    bspec = pl.BlockSpec((heads, N_KEYS, tt), lambda i: (0, 0, i))
    sspec = pl.BlockSpec((heads, tt), lambda i: (0, i))
    return pl.pallas_call(
        functools.partial(_peer_select_body, heads=heads),
        grid=(t // tt,),
        in_specs=[pl.BlockSpec((tt, q.shape[1]), lambda i: (i, 0)),
                  pl.BlockSpec(keys.shape, lambda i: (0, 0, 0))],
        out_specs=[bspec, bspec, bspec, sspec, sspec],
        out_shape=[big, big, big, small, small],
        compiler_params=_cparams(("parallel",)),
        name="peer_select",
    )(q, keys)


def _peer_dense_body(h_ref, u_ref, v_ref, s1_ref, s2_ref, e2_ref, tau_ref, m1_ref, o_ref, *, heads):
    j = pl.program_id(1)

    @pl.when(j == 0)
    def _():
        o_ref[...] = jnp.zeros_like(o_ref)

    te = u_ref.shape[0]
    act = lax.dot_general(u_ref[...], h_ref[...], (((1,), (1,)), ((), ())), preferred_element_type=F32)
    na = te // N_KEYS
    rows = []
    for al in range(na):
        a = j * na + al
        g = None
        for h in range(heads):
            s1row = s1_ref[h, pl.ds(a, 1), :]
            e1row = jnp.exp(s1row - m1_ref[h:h + 1, :])
            sel = (s1row + s2_ref[h]) >= tau_ref[h:h + 1, :]
            contrib = jnp.where(sel, e1row * e2_ref[h], 0.0)
            g = contrib if g is None else g + contrib
        rows.append(g)
    gate = jnp.concatenate(rows, axis=0)
    wgt = (gate * _gelu_tanh(act)).astype(BF16)
    o_ref[...] += lax.dot_general(wgt, v_ref[...], (((0,), (0,)), ((), ())), preferred_element_type=F32)


def peer_dense(h, u_tab, v_tab, s1, s2, e2, tau, m1, heads, tm=512, te=512):
    t, d = h.shape
    e = u_tab.shape[0]
    tm = _tile(tm, t)
    te = _tile(te, e)
    bspec = pl.BlockSpec((heads, N_KEYS, tm), lambda i, j: (0, 0, i))
    sspec = pl.BlockSpec((heads, tm), lambda i, j: (0, i))
    return pl.pallas_call(
        functools.partial(_peer_dense_body, heads=heads),
        grid=(t // tm, e // te),
        in_specs=[pl.BlockSpec((tm, d), lambda i, j: (i, 0)),
                  pl.BlockSpec((te, d), lambda i, j: (j, 0)),
                  pl.BlockSpec((te, d), lambda i, j: (j, 0)),
                  bspec, bspec, bspec, sspec, sspec],
        out_specs=pl.BlockSpec((tm, d), lambda i, j: (i, 0)),
        out_shape=jax.ShapeDtypeStruct((t, d), F32),
        compiler_params=_cparams(("parallel", "arbitrary")),
        name="peer_dense",
    )(h, u_tab, v_tab, s1, s2, e2, tau, m1)


def _rope_table(seqs):
    inv = 1.0 / (ROPE_BASE ** (jnp.arange(0, QK_ROPE, 2, dtype=F32) / QK_ROPE))
    pos = jnp.concatenate([jnp.arange(b - a, dtype=F32) for a, b in seqs])
    ang = pos[:, None] * inv[None, :]
    c, s = jnp.cos(ang), jnp.sin(ang)
    return jnp.concatenate([c, c, s, s], axis=-1)


def _rotate_half_cols(w):
    half = w.shape[-1] // 2
    return jnp.concatenate([-w[..., half:], w[..., :half]], axis=-1)


def _mixer(x, seqs, cs, norm_mix, w_in, pool_w, pool_scale, q_norm, w_uq, kv_norm, w_ukv,
           a_re, a_im, log_dt, b_re, b_im, c_re, c_im, d_skip, w_glu, p_a, p_b, p_c, w_o):
    t, d = x.shape
    pool_width = pool_w.shape[0] * pool_w.shape[1]
    q_lora = w_uq.shape[0]
    kv_lora = w_ukv.shape[0]
    heads = w_uq.shape[1]
    ssm_width = d_skip.shape[0]
    o1 = pool_width
    o2 = o1 + q_lora
    o3 = o2 + kv_lora
    o4 = o3 + QK_ROPE
    o5 = o4 + ssm_width

    w_kpe = w_in[:, o3:o4]
    small_cols = [w_in[:, :o1], w_in[:, o1:o2], w_in[:, o4:o5], w_in[:, o2:o3], w_kpe, _rotate_half_cols(w_kpe)]
    n_small = o5 + QK_ROPE
    tn_small = 1280
    n_pad = -n_small % tn_small
    w_small = jnp.concatenate(small_cols + [jnp.zeros((d, n_pad), w_in.dtype)], axis=1).astype(BF16)
    hn = rmsnorm(x, norm_mix, BF16)
    zs = matmul(hn, w_small, F32, tm=512, tn=tn_small, name="in_proj_small")
    gates = matmul(hn, w_in[:, o5:].astype(BF16), BF16, tm=1024, tn=1024, act="sigmoid", name="in_proj_gates")
    cq_block = o1 // q_lora
    ssm_block = o2 // ssm_width
    ckv_block = (o2 + ssm_width) // kv_lora
    kpe_block = (o2 + ssm_width + kv_lora) // LANES
    assert o1 % q_lora == 0 and o2 % ssm_width == 0 and (o2 + ssm_width) % kv_lora == 0

    ya = pool_mixer(zs, pool_w, pool_scale, seqs)

    wq_nope = w_uq[:, :, :QK_NOPE]
    wq_pe = w_uq[:, :, QK_NOPE:]
    wq = jnp.concatenate([wq_nope, wq_pe, _rotate_half_cols(wq_pe)], axis=-1).reshape(q_lora, heads * HEAD_PAD)
    qp = q_proj(zs, cq_block, q_norm, wq.astype(BF16), cs, heads)
    wk = w_ukv[:, :, :QK_NOPE].reshape(kv_lora, heads * QK_NOPE).astype(BF16)
    wv = w_ukv[:, :, QK_NOPE:].reshape(kv_lora, heads * V_HEAD).astype(BF16)
    kp, v = kv_proj(zs, ckv_block, kpe_block, kv_norm, wk, wv, cs, heads)
    outs = []
    groups = []
    for a, b in seqs:
        if groups and groups[-1][2] == b - a and groups[-1][0] + groups[-1][1] * groups[-1][2] == a:
            groups[-1][1] += 1
        else:
            groups.append([a, 1, b - a])
    for row0, nseq, seq in groups:
        outs.append(attention(qp, kp, v, row0, nseq, seq, heads))
    yb = jnp.concatenate(outs, axis=0) if len(outs) > 1 else outs[0]

    u_ssm = lax.slice_in_dim(zs, ssm_block * ssm_width, (ssm_block + 1) * ssm_width, axis=1)
    s5w = _s5_weights(a_re, a_im, log_dt, b_re, b_im, c_re, c_im, d_skip)
    yc = glu(s5_mix(u_ssm, s5w, seqs), w_glu.astype(BF16))

    merged = gated_merge(ya, yb, yc, p_a.astype(BF16), p_b.astype(BF16), p_c.astype(BF16), gates)
    return matmul(merged, w_o.astype(BF16), F32, tm=1024, tn=512, residual=x, name="out_proj")


def _peer(x, norm_ffn, peer_wq, peer_keys, peer_u, peer_v):
    h2 = rmsnorm(x, norm_ffn, BF16)
    pheads = peer_keys.shape[0]
    q = matmul(h2, peer_wq.astype(BF16), F32, tm=1024, tn=1024, name="peer_query")
    keys = peer_keys.reshape(pheads * 2, N_KEYS, peer_keys.shape[-1]).astype(F32)
    s1, s2, e2, tau, m1 = peer_select(q, keys, pheads)
    ffn = peer_dense(h2, peer_u.astype(BF16), peer_v.astype(BF16), s1, s2, e2, tau, m1, pheads)
    return x + ffn


def kernel(x_prompt, x_sample, norm_mix, w_in, pool_w, pool_scale, q_norm, w_uq, kv_norm, w_ukv, ssm_a_re, ssm_a_im, ssm_log_dt, ssm_b_re, ssm_b_im, ssm_c_re, ssm_c_im, ssm_d, w_glu, p_a, p_b, p_c, w_o, norm_ffn, peer_wq, peer_keys, peer_u, peer_v, final_norm):
    d = x_prompt.shape[-1]
    seqs = []
    row = 0
    for xs in (x_prompt, x_sample):
        for _ in range(xs.shape[0]):
            seqs.append((row, row + xs.shape[1]))
            row += xs.shape[1]
    x = jnp.concatenate([x_prompt.reshape(-1, d), x_sample.reshape(-1, d)], axis=0)
    cs = _rope_table(seqs)
    per_layer = (norm_mix, w_in, pool_w, pool_scale, q_norm, w_uq, kv_norm, w_ukv, ssm_a_re, ssm_a_im, ssm_log_dt,
                 ssm_b_re, ssm_b_im, ssm_c_re, ssm_c_im, ssm_d, w_glu, p_a, p_b, p_c, w_o, norm_ffn, peer_wq,
                 peer_keys, peer_u, peer_v)
    n_mix = 21
    for layer in range(norm_mix.shape[0]):
        lw = [w[layer] for w in per_layer]
        x = _mixer(x, seqs, cs, *lw[:n_mix])
        x = _peer(x, *lw[n_mix:])
    y = rmsnorm(x, final_norm, x_prompt.dtype)
    n_prompt = x_prompt.shape[0] * x_prompt.shape[1]
    return (y[:n_prompt].reshape(x_prompt.shape), y[n_prompt:].reshape(x_sample.shape))
```

```python
import functools
import math

import jax
import jax.numpy as jnp
from jax import lax
from jax.experimental import pallas as pl
from jax.experimental.pallas import tpu as pltpu

F32 = jnp.float32
BF16 = jnp.bfloat16

EPS = 1e-6
POOL_WINDOWS = (2, 4, 8, 16)
QK_NOPE = 128
QK_ROPE = 64
V_HEAD = 128
ROPE_BASE = 10000.0
SSM_GROUP = 16
SSM_STATE = 64
PEER_TOPK = 16
N_KEYS = 128

LANES = 128
SUBLANES = 8
HEAD_PAD = 256
SSM_CHUNK = 16
VMEM_LIMIT = 56 * 1024 * 1024
POOL_HALO = 8


def _cparams(sem, flags=None):
    return pltpu.CompilerParams(dimension_semantics=sem, vmem_limit_bytes=VMEM_LIMIT, flags=flags)


def _tile(pref, dim):
    t = min(pref, dim)
    assert dim % t == 0, (pref, dim)
    return t


def _gelu_tanh(x):
    cdf = 0.5 * (1.0 + jnp.tanh(math.sqrt(2.0 / math.pi) * (x + 0.044715 * (x * x * x))))
    return x * cdf


def _rms(x, g):
    y = x * lax.rsqrt(jnp.mean(x * x, axis=-1, keepdims=True) + EPS)
    return y * g


def _rmsnorm_body(x_ref, g_ref, o_ref):
    o_ref[...] = _rms(x_ref[...].astype(F32), g_ref[...]).astype(o_ref.dtype)


def rmsnorm(x, g, out_dtype, tm=512, row0=0, nrows=None):
    d = x.shape[1]
    t = x.shape[0] if nrows is None else nrows
    tm = _tile(tm, math.gcd(t, row0) if row0 else t)
    b0 = row0 // tm
    return pl.pallas_call(
        _rmsnorm_body,
        grid=(t // tm,),
        in_specs=[pl.BlockSpec((tm, d), lambda i: (b0 + i, 0)),
                  pl.BlockSpec((1, d), lambda i: (0, 0))],
        out_specs=pl.BlockSpec((tm, d), lambda i: (i, 0)),
        out_shape=jax.ShapeDtypeStruct((t, d), out_dtype),
        compiler_params=_cparams(("parallel",)),
        name="rmsnorm",
    )(x, g.reshape(1, d).astype(F32))


def _mm_body(a_ref, b_ref, o_ref, *, act):
    acc = jnp.dot(a_ref[...], b_ref[...], preferred_element_type=F32)
    if act == "sigmoid":
        acc = jax.nn.sigmoid(acc)
    o_ref[...] = acc.astype(o_ref.dtype)


def _mm_res_body(a_ref, b_ref, r_ref, o_ref):
    acc = jnp.dot(a_ref[...], b_ref[...], preferred_element_type=F32)
    o_ref[...] = (r_ref[...] + acc).astype(o_ref.dtype)


def matmul(a, b, out_dtype, tm, tn, act=None, residual=None, name="matmul"):
    m, k = a.shape
    n = b.shape[1]
    tm = _tile(tm, m)
    tn = _tile(tn, n)
    in_specs = [pl.BlockSpec((tm, k), lambda i, j: (i, 0)),
                pl.BlockSpec((k, tn), lambda i, j: (0, j))]
    args = [a, b]
    if residual is None:
        body = functools.partial(_mm_body, act=act)
    else:
        body = _mm_res_body
        in_specs.append(pl.BlockSpec((tm, tn), lambda i, j: (i, j)))
        args.append(residual)
    return pl.pallas_call(
        body,
        grid=(m // tm, n // tn),
        in_specs=in_specs,
        out_specs=pl.BlockSpec((tm, tn), lambda i, j: (i, j)),
        out_shape=jax.ShapeDtypeStruct((m, n), out_dtype),
        compiler_params=_cparams(("parallel", "parallel")),
        name=name,
    )(*args)


def _seq_bounds(r0, seqs):
    s0 = jnp.int32(0)
    s1 = jnp.int32(0)
    for a, b in seqs:
        inside = jnp.logical_and(r0 >= a, r0 < b)
        s0 = jnp.where(inside, jnp.int32(a), s0)
        s1 = jnp.where(inside, jnp.int32(b), s1)
    return s0, s1


def _pool_body(prev_ref, cur_ref, next_ref, w_ref, sc_ref, o_ref, *, ts, seqs, group):
    r0 = pl.program_id(0) * ts
    s0, s1 = _seq_bounds(r0, seqs)
    cur = cur_ref[...].astype(F32)
    prev = jnp.where(r0 > s0, prev_ref[...].astype(F32), 0.0)
    nxt = jnp.where(r0 + ts < s1, next_ref[...].astype(F32), 0.0)
    ext = jnp.concatenate([prev, cur, nxt], axis=0).astype(BF16)
    te = ts + 2 * POOL_HALO
    d = lax.broadcasted_iota(jnp.int32, (ts, te), 1) - lax.broadcasted_iota(jnp.int32, (ts, te), 0)
    rows = r0 + lax.broadcasted_iota(jnp.int32, (ts, group), 0)
    for gi, w in enumerate(POOL_WINDOWS):
        left = w // 2
        right = w - 1 - left
        band = jnp.where(d >= POOL_HALO - left, jnp.where(d <= POOL_HALO + right, 1.0, 0.0), 0.0).astype(BF16)
        sl = slice(gi * group, (gi + 1) * group)
        tot = jnp.dot(band, ext[:, sl], preferred_element_type=F32)
        lo = jnp.maximum(rows - left, s0)
        hi = jnp.minimum(rows + right, s1 - 1)
        cnt = (hi - lo + 1).astype(F32)
        p = tot / cnt - cur[:, sl]
        y = jnp.dot(p.astype(BF16), w_ref[gi], preferred_element_type=F32) * sc_ref[:, sl]
        o_ref[:, sl] = y.astype(o_ref.dtype)


def pool_mixer(zs, pool_w, pool_scale, seqs, ts=512):
    t = zs.shape[0]
    ng, group, _ = pool_w.shape
    width = ng * group
    ts = _tile(ts, min(b - a for a, b in seqs))
    hb = ts // POOL_HALO
    nrow8 = t // POOL_HALO
    body = functools.partial(_pool_body, ts=ts, seqs=tuple(seqs), group=group)
    return pl.pallas_call(
        body,
        grid=(t // ts,),
        in_specs=[pl.BlockSpec((POOL_HALO, width), lambda i: (jnp.maximum(i * hb - 1, 0), 0)),
                  pl.BlockSpec((ts, width), lambda i: (i, 0)),
                  pl.BlockSpec((POOL_HALO, width), lambda i: (jnp.minimum((i + 1) * hb, nrow8 - 1), 0)),
                  pl.BlockSpec((ng, group, group), lambda i: (0, 0, 0)),
                  pl.BlockSpec((1, width), lambda i: (0, 0))],
        out_specs=pl.BlockSpec((ts, width), lambda i: (i, 0)),
        out_shape=jax.ShapeDtypeStruct((t, width), BF16),
        compiler_params=_cparams(("parallel",)),
        name="pool_mixer",
    )(zs, zs, zs, pool_w.astype(BF16), pool_scale.reshape(1, width).astype(F32))


def _rope_combine(x, cs):
    tmp = x * cs
    return tmp + pltpu.roll(tmp, LANES // 2, axis=1)


def _qproj_body(z_ref, g_ref, w_ref, cs_ref, o_ref, *, heads, scale):
    cn = _rms(z_ref[...].astype(F32), g_ref[...]).astype(BF16)
    acc = jnp.dot(cn, w_ref[...], preferred_element_type=F32)
    cs = cs_ref[...] * scale
    for h in range(heads):
        lo = acc[:, h * HEAD_PAD:h * HEAD_PAD + LANES] * scale
        hi = _rope_combine(acc[:, h * HEAD_PAD + LANES:(h + 1) * HEAD_PAD], cs)
        o_ref[:, h * HEAD_PAD:h * HEAD_PAD + LANES] = lo.astype(o_ref.dtype)
        o_ref[:, h * HEAD_PAD + LANES:(h + 1) * HEAD_PAD] = hi.astype(o_ref.dtype)


def q_proj(zs, col_block, q_norm, wq, cs, heads, tm=512):
    t = zs.shape[0]
    r = wq.shape[0]
    tm = _tile(tm, t)
    scale = (QK_NOPE + QK_ROPE) ** -0.5 * math.log2(math.e)
    body = functools.partial(_qproj_body, heads=heads, scale=scale)
    return pl.pallas_call(
        body,
        grid=(t // tm,),
        in_specs=[pl.BlockSpec((tm, r), lambda i: (i, col_block)),
                  pl.BlockSpec((1, r), lambda i: (0, 0)),
                  pl.BlockSpec((r, heads * HEAD_PAD), lambda i: (0, 0)),
                  pl.BlockSpec((tm, LANES), lambda i: (i, 0))],
        out_specs=pl.BlockSpec((tm, heads * HEAD_PAD), lambda i: (i, 0)),
        out_shape=jax.ShapeDtypeStruct((t, heads * HEAD_PAD), BF16),
        compiler_params=_cparams(("parallel",)),
        name="q_proj",
    )(zs, q_norm.reshape(1, r).astype(F32), wq, cs)


def _kvproj_body(z_ref, g_ref, kpe_ref, wk_ref, wvt_ref, cs_ref, k_ref, vt_ref, *, heads):
    cn = _rms(z_ref[...].astype(F32), g_ref[...]).astype(BF16)
    kn = jnp.dot(cn, wk_ref[...], preferred_element_type=F32)
    vt = lax.dot_general(wvt_ref[...], cn, (((1,), (1,)), ((), ())), preferred_element_type=F32)
    kr = _rope_combine(kpe_ref[...].astype(F32), cs_ref[...])
    lane = lax.broadcasted_iota(jnp.int32, kr.shape, 1)
    kr = jnp.where(lane < QK_ROPE, kr, 0.0).astype(k_ref.dtype)
    for h in range(heads):
        k_ref[:, h * HEAD_PAD:h * HEAD_PAD + LANES] = kn[:, h * LANES:(h + 1) * LANES].astype(k_ref.dtype)
        k_ref[:, h * HEAD_PAD + LANES:(h + 1) * HEAD_PAD] = kr
    vt_ref[...] = vt.astype(vt_ref.dtype)


def kv_proj(zs, ckv_block, kpe_block, kv_norm, wk, wvt, cs, heads, tm=512):
    t = zs.shape[0]
    r = wk.shape[0]
    tm = _tile(tm, t)
    body = functools.partial(_kvproj_body, heads=heads)
    return pl.pallas_call(
        body,
        grid=(t // tm,),
        in_specs=[pl.BlockSpec((tm, r), lambda i: (i, ckv_block)),
                  pl.BlockSpec((1, r), lambda i: (0, 0)),
                  pl.BlockSpec((tm, LANES), lambda i: (i, kpe_block)),
                  pl.BlockSpec((r, heads * QK_NOPE), lambda i: (0, 0)),
                  pl.BlockSpec((heads * V_HEAD, r), lambda i: (0, 0)),
                  pl.BlockSpec((tm, LANES), lambda i: (i, 0))],
        out_specs=[pl.BlockSpec((tm, heads * HEAD_PAD), lambda i: (i, 0)),
                   pl.BlockSpec((heads * V_HEAD, tm), lambda i: (0, i))],
        out_shape=[jax.ShapeDtypeStruct((t, heads * HEAD_PAD), BF16),
                   jax.ShapeDtypeStruct((heads * V_HEAD, t), BF16)],
        compiler_params=_cparams(("parallel",)),
        name="kv_proj",
    )(zs, kv_norm.reshape(1, r).astype(F32), zs, wk, wvt, cs)


def _sublane_allreduce(x, op):
    for shift in (4, 2, 1):
        x = op(x, pltpu.roll(x, shift, axis=0))
    return x


def _attn_body(q_ref, k_ref, vt_ref, o_ref, s_scr, smax_scr, *, tk, nsplit):
    tq = q_ref.shape[0]
    tw = tq // nsplit
    nk = k_ref.shape[0] // tk
    vr = V_HEAD // SUBLANES
    qt = q_ref[...].astype(F32).T.astype(BF16)
    qts = [qt[:, i * tw:(i + 1) * tw] for i in range(nsplit)]

    def produce(j, slot):
        ks = k_ref[pl.ds(pl.multiple_of(j * tk, tk), tk), :]
        for i, qh in enumerate(qts):
            s = jnp.dot(ks, qh, preferred_element_type=F32).reshape(tk // SUBLANES, SUBLANES, tw)
            s_scr[slot, i] = s
            smax_scr[slot, i] = jnp.max(s, axis=0)

    def consume(j, slot, state):
        vts = vt_ref[:, pl.ds(pl.multiple_of(j * tk, tk), tk)]
        new = []
        for i, (m, l, acc) in enumerate(state):
            mn = jnp.maximum(m, _sublane_allreduce(smax_scr[slot, i], jnp.maximum))
            p = jnp.exp2(s_scr[slot, i] - mn[None])
            alpha = jnp.exp2(m - mn)
            l = alpha * l + jnp.sum(p, axis=0)
            pv = jnp.dot(vts, p.reshape(tk, tw).astype(BF16), preferred_element_type=F32)
            acc = alpha[None] * acc + pv.reshape(vr, SUBLANES, tw)
            new.append((mn, l, acc))
        return tuple(new)

    def pair(jj, state):
        j = 2 * jj
        produce(j + 1, 1)
        state = consume(j, 0, state)
        produce(j + 2, 0)
        return consume(j + 1, 1, state)

    state = tuple((jnp.full((SUBLANES, tw), -jnp.inf, F32), jnp.zeros((SUBLANES, tw), F32),
                   jnp.zeros((vr, SUBLANES, tw), F32)) for _ in range(nsplit))
    produce(0, 0)
    if nk == 1:
        fin = consume(0, 0, state)
    else:
        assert nk % 2 == 0
        state = lax.fori_loop(0, nk // 2 - 1, pair, state)
        produce(nk - 1, 1)
        fin = consume(nk - 1, 1, consume(nk - 2, 0, state))
    for i, (_, l, acc) in enumerate(fin):
        lt = _sublane_allreduce(l, jnp.add)
        o = (acc / lt[None]).reshape(V_HEAD, tw)
        o_ref[i * tw:(i + 1) * tw, :] = o.T.astype(o_ref.dtype)


def _attn_alias_body(q_ref, k_ref, vt_ref, prev_ref, o_ref, s_scr, smax_scr, *, tk, nsplit):
    del prev_ref
    _attn_body(q_ref, k_ref, vt_ref, o_ref, s_scr, smax_scr, tk=tk, nsplit=nsplit)


def attention(qp, kp, vt, row0, nseq, seq, heads, prev=None, tq=512, tk=512, nsplit=2):
    t = qp.shape[0]
    tq = _tile(tq, seq)
    tk = _tile(tk, seq)
    nq = seq // tq
    q0 = row0 // tq
    b0 = row0 // seq
    assert row0 % seq == 0
    in_specs = [pl.BlockSpec((tq, HEAD_PAD), lambda b, h, i: (q0 + b * nq + i, h)),
                pl.BlockSpec((seq, HEAD_PAD), lambda b, h, i: (b0 + b, h)),
                pl.BlockSpec((V_HEAD, seq), lambda b, h, i: (h, b0 + b))]
    args = [qp, kp, vt]
    if prev is None:
        body, aliases = _attn_body, {}
    else:
        body, aliases = _attn_alias_body, {3: 0}
        in_specs.append(pl.BlockSpec(memory_space=pl.ANY))
        args.append(prev)
    return pl.pallas_call(
        functools.partial(body, tk=tk, nsplit=nsplit),
        grid=(nseq, heads, nq),
        in_specs=in_specs,
        out_specs=pl.BlockSpec((tq, V_HEAD), lambda b, h, i: (q0 + b * nq + i, h)),
        out_shape=jax.ShapeDtypeStruct((t, heads * V_HEAD), BF16),
        scratch_shapes=[pltpu.VMEM((2, nsplit, tk // SUBLANES, SUBLANES, tq // nsplit), F32),
                        pltpu.VMEM((2, nsplit, SUBLANES, tq // nsplit), F32)],
        input_output_aliases=aliases,
        compiler_params=_cparams(("parallel", "parallel", "parallel")),
        name="mla_attention",
    )(*args)


def _s5_weights(a_re, a_im, log_dt, b_re, b_im, c_re, c_im, d_skip):
    L = SSM_CHUNK
    lam = lax.complex(a_re.astype(F32), a_im.astype(F32))
    dt = jnp.exp(log_dt.astype(F32))[..., None]
    z = lam * dt
    lam_bar = jnp.exp(z)
    ks = jnp.arange(L + 1, dtype=F32)
    pw = jnp.exp(z[..., None] * ks.astype(jnp.complex64))
    b_bar = ((lam_bar - 1.0) / lam)[..., None] * lax.complex(b_re.astype(F32), b_im.astype(F32))
    c_mat = lax.complex(c_re.astype(F32), c_im.astype(F32))
    g, n, p = b_bar.shape[1:]
    kern = jnp.einsum("dgpn,dgnk,dgnq->dgkpq", c_mat, pw[..., :L], b_bar).real
    li = jnp.arange(L)
    diff = li[None, :] - li[:, None]
    kf = jnp.take(kern[0], jnp.clip(diff, 0, L - 1), axis=1)
    kb = jnp.take(kern[1], jnp.clip(-diff, 0, L - 1), axis=1)
    dm = diff[None, :, :, None, None]
    eye = jnp.eye(p, dtype=F32) * d_skip.astype(F32).reshape(g, 1, 1, p, 1)
    m = jnp.where(dm > 0, kf, 0.0) + jnp.where(dm < 0, kb, 0.0) + jnp.where(dm == 0, kf + kb + eye, 0.0)
    m = m.transpose(0, 1, 4, 2, 3).reshape(g, L * p, L * p)

    cf = jnp.einsum("gnl,gnq->glqn", pw[0][..., L - 1 - li], b_bar[0])
    cb = jnp.einsum("gnl,gnq->glqn", pw[1][..., li], b_bar[1])
    cf = cf.reshape(g, L * p, n)
    cb = cb.reshape(g, L * p, n)
    bx = jnp.concatenate([cf.real, cf.imag, cf.imag, cf.real, cb.real, cb.imag, cb.imag, cb.real], axis=-1)

    wf = jnp.einsum("gpn,gnl->gnlp", c_mat[0], pw[0][..., li + 1]).reshape(g, n, L * p)
    wb = jnp.einsum("gpn,gnl->gnlp", c_mat[1], pw[1][..., L - li]).reshape(g, n, L * p)
    cyf = jnp.concatenate([wf.real, -wf.imag], axis=1)
    cyb = jnp.concatenate([wb.real, -wb.imag], axis=1)

    al = pw[..., L]

    def pack(x, y):
        return jnp.concatenate([x, y], axis=-1).reshape(1, g * 2 * n)

    a1f, a2f = pack(al[0].real, al[0].real), pack(-al[0].imag, al[0].imag)
    a1b, a2b = pack(al[1].real, al[1].real), pack(-al[1].imag, al[1].imag)
    return (m.astype(BF16), bx.astype(BF16), cyf.astype(BF16), cyb.astype(BF16),
            a1f.astype(F32), a2f.astype(F32), a1b.astype(F32), a2b.astype(F32))


def _s5x_body(u_ref, bx_ref, xf_ref, xfs_ref, xb_ref, xbs_ref):
    x = jnp.dot(u_ref[0], bx_ref[0], preferred_element_type=F32)
    xf_ref[...] = x[:, 0 * LANES:1 * LANES]
    xfs_ref[...] = x[:, 1 * LANES:2 * LANES]
    xb_ref[...] = x[:, 2 * LANES:3 * LANES]
    xbs_ref[...] = x[:, 3 * LANES:4 * LANES]


def _s5scan_body(xf_ref, xfs_ref, xb_ref, xbs_ref, a1f_ref, a2f_ref, a1b_ref, a2b_ref, hf_ref, hb_ref, *,
                 seq_chunks):
    a1f, a2f, a1b, a2b = a1f_ref[...], a2f_ref[...], a1b_ref[...], a2b_ref[...]
    zero = jnp.zeros_like(a1f)
    for c0, n in seq_chunks:

        def step(i, carry, c0=c0, n=n):
            hf, hfs, hb, hbs = carry
            rf = c0 + i
            rb = c0 + n - 1 - i
            hf_ref[pl.ds(rf, 1), :] = hf
            hb_ref[pl.ds(rb, 1), :] = hb
            xf = xf_ref[pl.ds(rf, 1), :]
            xfs = xfs_ref[pl.ds(rf, 1), :]
            xb = xb_ref[pl.ds(rb, 1), :]
            xbs = xbs_ref[pl.ds(rb, 1), :]
            nhf = a1f * hf + a2f * hfs + xf
            nhfs = a1f * hfs - a2f * hf + xfs
            nhb = a1b * hb + a2b * hbs + xb
            nhbs = a1b * hbs - a2b * hb + xbs
            return nhf, nhfs, nhb, nhbs

        lax.fori_loop(0, n, step, (zero, zero, zero, zero))


def _s5y_body(u_ref, m_ref, hf_ref, hb_ref, cyf_ref, cyb_ref, y_ref):
    y = jnp.dot(u_ref[0], m_ref[0], preferred_element_type=F32)
    y = y + jnp.dot(hf_ref[...].astype(BF16), cyf_ref[0], preferred_element_type=F32)
    y = y + jnp.dot(hb_ref[...].astype(BF16), cyb_ref[0], preferred_element_type=F32)
    y_ref[0] = y


def s5_mix(u, weights, seqs):
    m, bx, cyf, cyb, a1f, a2f, a1b, a2b = weights
    t = u.shape[0]
    g = m.shape[0]
    L, p = SSM_CHUNK, SSM_GROUP
    lp = L * p
    c = t // L
    ug = u.reshape(c, L, g, p).transpose(2, 0, 1, 3).reshape(g, c, lp).astype(BF16)
    xs = pl.pallas_call(
        _s5x_body,
        grid=(g,),
        in_specs=[pl.BlockSpec((1, c, lp), lambda i: (i, 0, 0)),
                  pl.BlockSpec((1, lp, 4 * LANES), lambda i: (i, 0, 0))],
        out_specs=[pl.BlockSpec((c, LANES), lambda i: (0, i))] * 4,
        out_shape=[jax.ShapeDtypeStruct((c, g * LANES), F32)] * 4,
        compiler_params=_cparams(("parallel",)),
        name="s5_chunk_states",
    )(ug, bx)
    seq_chunks = tuple((a // L, (b - a) // L) for a, b in seqs)
    tc = _tile(512, g * LANES)
    xspec = pl.BlockSpec((c, tc), lambda i: (0, i))
    aspec = pl.BlockSpec((1, tc), lambda i: (0, i))
    hf, hb = pl.pallas_call(
        functools.partial(_s5scan_body, seq_chunks=seq_chunks),
        grid=(g * LANES // tc,),
        in_specs=[xspec] * 4 + [aspec] * 4,
        out_specs=[xspec] * 2,
        out_shape=[jax.ShapeDtypeStruct((c, g * LANES), F32)] * 2,
        compiler_params=_cparams(("parallel",)),
        name="s5_state_scan",
    )(*xs, a1f, a2f, a1b, a2b)
    yg = pl.pallas_call(
        _s5y_body,
        grid=(g,),
        in_specs=[pl.BlockSpec((1, c, lp), lambda i: (i, 0, 0)),
                  pl.BlockSpec((1, lp, lp), lambda i: (i, 0, 0)),
                  pl.BlockSpec((c, LANES), lambda i: (0, i)),
                  pl.BlockSpec((c, LANES), lambda i: (0, i)),
                  pl.BlockSpec((1, LANES, lp), lambda i: (i, 0, 0)),
                  pl.BlockSpec((1, LANES, lp), lambda i: (i, 0, 0))],
        out_specs=pl.BlockSpec((1, c, lp), lambda i: (i, 0, 0)),
        out_shape=jax.ShapeDtypeStruct((g, c, lp), F32),
        compiler_params=_cparams(("parallel",)),
        name="s5_outputs",
    )(ug, m, hf, hb, cyf, cyb)
    return yg.reshape(g, c, L, p).transpose(1, 2, 0, 3).reshape(t, g * p)


def _glu_body(y_ref, w_ref, o_ref):
    gl = _gelu_tanh(y_ref[...])
    zz = jnp.dot(gl.astype(BF16), w_ref[...], preferred_element_type=F32)
    o_ref[...] = (gl * jax.nn.sigmoid(zz)).astype(o_ref.dtype)


def glu(y, w_glu, tm=512):
    t, w = y.shape
    tm = _tile(tm, t)
    return pl.pallas_call(
        _glu_body,
        grid=(t // tm,),
        in_specs=[pl.BlockSpec((tm, w), lambda i: (i, 0)),
                  pl.BlockSpec((w, w), lambda i: (0, 0))],
        out_specs=pl.BlockSpec((tm, w), lambda i: (i, 0)),
        out_shape=jax.ShapeDtypeStruct((t, w), BF16),
        compiler_params=_cparams(("parallel",)),
        name="s5_glu",
    )(y, w_glu)


def _merge_body(ya_ref, yb_ref, yc_ref, pa_ref, pb_ref, pc_ref, ga_ref, gb_ref, gc_ref, o_ref):
    a = jnp.dot(ya_ref[...], pa_ref[...], preferred_element_type=F32)
    b = jnp.dot(yb_ref[...], pb_ref[...], preferred_element_type=F32)
    c = jnp.dot(yc_ref[...], pc_ref[...], preferred_element_type=F32)
    o = ga_ref[...].astype(F32) * a + gb_ref[...].astype(F32) * b + gc_ref[...].astype(F32) * c
    o_ref[...] = o.astype(o_ref.dtype)


def gated_merge(ya, yb, yc, pa, pb, pc, gates, tm=512, tn=1024):
    t = ya.shape[0]
    d = pa.shape[1]
    tm = _tile(tm, t)
    tn = _tile(tn, d)
    nj = d // tn

    def aspec(x):
        return pl.BlockSpec((tm, x.shape[1]), lambda i, j: (i, 0))

    def wspec(x):
        return pl.BlockSpec((x.shape[0], tn), lambda i, j: (0, j))

    gspecs = [pl.BlockSpec((tm, tn), functools.partial(lambda i, j, k: (i, k * nj + j), k=k)) for k in range(3)]
    return pl.pallas_call(
        _merge_body,
        grid=(t // tm, nj),
        in_specs=[aspec(ya), aspec(yb), aspec(yc), wspec(pa), wspec(pb), wspec(pc)] + gspecs,
        out_specs=pl.BlockSpec((tm, tn), lambda i, j: (i, j)),
        out_shape=jax.ShapeDtypeStruct((t, d), BF16),
        compiler_params=_cparams(("parallel", "parallel")),
        name="gated_merge",
    )(ya, yb, yc, pa, pb, pc, gates, gates, gates)


def _oddeven_merge(lo, hi, r):
    step = r * 2
    if step < hi - lo:
        yield from _oddeven_merge(lo, hi, step)
        yield from _oddeven_merge(lo + r, hi, step)
        yield from [(i, i + r) for i in range(lo + r, hi - r, step)]
    else:
        yield (lo, lo + r)


def _oddeven_merge_sort(lo, hi):
    if hi - lo >= 1:
        mid = lo + (hi - lo) // 2
        yield from _oddeven_merge_sort(lo, mid)
        yield from _oddeven_merge_sort(mid + 1, hi)
        yield from _oddeven_merge(lo, hi, 1)


_SORT16 = tuple(_oddeven_merge_sort(0, PEER_TOPK - 1))
_BITONIC16 = tuple((i, i + d) for d in (8, 4, 2, 1) for i in range(PEER_TOPK) if not i & d)


def _compare_exchange(v, pairs):
    for i, j in pairs:
        hi = jnp.maximum(v[i], v[j])
        lo = jnp.minimum(v[i], v[j])
        v[i], v[j] = hi, lo


def _top16_desc(v):
    v = list(v)
    _compare_exchange(v, _SORT16)
    for shift in (4, 2, 1):
        r = [pltpu.roll(x, shift, axis=0) for x in v]
        v = [jnp.maximum(v[i], r[PEER_TOPK - 1 - i]) for i in range(PEER_TOPK)]
        _compare_exchange(v, _BITONIC16)
    return v


def _peer_select_body(q_ref, keys_ref, s1_ref, s2_ref, e2_ref, tau_ref, m1_ref, *, heads):
    tt = q_ref.shape[0]
    sub = lax.broadcasted_iota(jnp.int32, (SUBLANES, tt), 0)
    ninf = jnp.full((SUBLANES, tt), -jnp.inf, F32)

    def by_sublane(lst):
        out = lst[SUBLANES - 1]
        for j in range(SUBLANES - 2, -1, -1):
            out = jnp.where(sub == j, lst[j], out)
        return out

    for h in range(heads):
        scores, tops = [], []
        for c in range(2):
            blk = 2 * h + c
            qb = q_ref[:, blk * N_KEYS:(blk + 1) * N_KEYS]
            kb = keys_ref[blk]
            q_hi, k_hi = qb.astype(BF16), kb.astype(BF16)
            q_lo = (qb - q_hi.astype(F32)).astype(BF16)
            k_lo = (kb - k_hi.astype(F32)).astype(BF16)
            nt = (((1,), (1,)), ((), ()))
            st = (lax.dot_general(k_hi, q_hi, nt, preferred_element_type=F32)
                  + lax.dot_general(k_hi, q_lo, nt, preferred_element_type=F32)
                  + lax.dot_general(k_lo, q_hi, nt, preferred_element_type=F32))
            scores.append(st)
            tops.append(_top16_desc([st[SUBLANES * a:SUBLANES * (a + 1), :] for a in range(N_KEYS // SUBLANES)]))
        t1, t2 = tops
        t2a, t2b, t1b = by_sublane(t2[:8]), by_sublane(t2[8:]), by_sublane(t1[8:])
        cand = [t1[0] + t2a, t1[0] + t2b, t1[1] + t2a,
                jnp.where(sub < 5, t1[2] + t2a, ninf), jnp.where(sub < 4, t1[3] + t2a, ninf),
                jnp.where(sub < 3, t1[4] + t2a, ninf), jnp.where(sub < 2, t1[5] + t2a, ninf),
                jnp.where(sub < 2, t1[6] + t2a, ninf), jnp.where(sub < 2, t1[7] + t2a, ninf),
                t1b + t2[0]] + [ninf] * 6
        best = _top16_desc(cand)
        z = jnp.exp(best[0] - best[0])
        for b in best[1:]:
            z = z + jnp.exp(b - best[0])
        s1_ref[h] = scores[0]
        s2_ref[h] = scores[1]
        e2_ref[h] = jnp.exp(scores[1] - t2[0][0:1, :]) / z[0:1, :]
        tau_ref[h:h + 1, :] = best[PEER_TOPK - 1][0:1, :]
        m1_ref[h:h + 1, :] = t1[0][0:1, :]


def peer_select(q, keys, heads, tt=256):
    t = q.shape[0]
    tt = _tile(tt, t)
    big = jax.ShapeDtypeStruct((heads, N_KEYS, t), F32)
    small = jax.ShapeDtypeStruct((heads, t), F32)
    bspec = pl.BlockSpec((heads, N_KEYS, tt), lambda i: (0, 0, i))
    sspec = pl.BlockSpec((heads, tt), lambda i: (0, i))
    return pl.pallas_call(
        functools.partial(_peer_select_body, heads=heads),
        grid=(t // tt,),
        in_specs=[pl.BlockSpec((tt, q.shape[1]), lambda i: (i, 0)),
                  pl.BlockSpec(keys.shape, lambda i: (0, 0, 0))],
        out_specs=[bspec, bspec, bspec, sspec, sspec],
        out_shape=[big, big, big, small, small],
        compiler_params=_cparams(("parallel",)),
        name="peer_select",
    )(q, keys)


def _peer_dense_body(x_ref, h_ref, u_ref, v_ref, s1_ref, s2_ref, e2_ref, tau_ref, m1_ref, o_ref, *, heads):
    j = pl.program_id(1)
    te = u_ref.shape[0]
    na = te // N_KEYS

    @pl.when(j == 0)
    def _():
        o_ref[...] = x_ref[...]

    act = lax.dot_general(u_ref[...], h_ref[...], (((1,), (1,)), ((), ())), preferred_element_type=F32)
    rows = []
    for al in range(na):
        a = j * na + al
        g = None
        for h in range(heads):
            s1row = s1_ref[h, pl.ds(a, 1), :]
            e1row = jnp.exp(s1row - m1_ref[h:h + 1, :])
            sel = (s1row + s2_ref[h]) >= tau_ref[h:h + 1, :]
            contrib = jnp.where(sel, e1row * e2_ref[h], 0.0)
            g = contrib if g is None else g + contrib
        rows.append(g)
    gate = rows[0] if na == 1 else jnp.concatenate(rows, axis=0)
    wgt = (gate * _gelu_tanh(act)).astype(BF16)
    o_ref[...] += lax.dot_general(wgt, v_ref[...], (((0,), (0,)), ((), ())), preferred_element_type=F32)


def peer_dense(x, h, u_tab, v_tab, layer, s1, s2, e2, tau, m1, heads, tm=512, te=512):
    t, d = h.shape
    e = u_tab.shape[1]
    tm = _tile(tm, t)
    te = _tile(te, e)
    once = pl.Buffered(1)
    bspec = pl.BlockSpec((heads, N_KEYS, tm), lambda i, j: (0, 0, i), pipeline_mode=once)
    sspec = pl.BlockSpec((heads, tm), lambda i, j: (0, i))
    return pl.pallas_call(
        functools.partial(_peer_dense_body, heads=heads),
        grid=(t // tm, e // te),
        in_specs=[pl.BlockSpec((tm, d), lambda i, j: (i, 0), pipeline_mode=once),
                  pl.BlockSpec((tm, d), lambda i, j: (i, 0), pipeline_mode=once),
                  pl.BlockSpec((None, te, d), lambda i, j: (layer, j, 0)),
                  pl.BlockSpec((None, te, d), lambda i, j: (layer, j, 0)),
                  bspec, bspec, bspec, sspec, sspec],
        out_specs=pl.BlockSpec((tm, d), lambda i, j: (i, 0)),
        out_shape=jax.ShapeDtypeStruct((t, d), F32),
        compiler_params=_cparams(("parallel", "arbitrary")),
        name="peer_dense",
    )(x, h, u_tab, v_tab, s1, s2, e2, tau, m1)


def _rope_table(seqs):
    inv = 1.0 / (ROPE_BASE ** (jnp.arange(0, QK_ROPE, 2, dtype=F32) / QK_ROPE))
    pos = jnp.concatenate([jnp.arange(b - a, dtype=F32) for a, b in seqs])
    ang = pos[:, None] * inv[None, :]
    c, s = jnp.cos(ang), jnp.sin(ang)
    return jnp.concatenate([c, c, s, s], axis=-1)


def _rotate_half_cols(w):
    half = w.shape[-1] // 2
    return jnp.concatenate([-w[..., half:], w[..., :half]], axis=-1)


def _mixer(x, seqs, cs, norm_mix, w_in, pool_w, pool_scale, q_norm, w_uq, kv_norm, w_ukv,
           a_re, a_im, log_dt, b_re, b_im, c_re, c_im, d_skip, w_glu, p_a, p_b, p_c, w_o):
    t, d = x.shape
    pool_width = pool_w.shape[0] * pool_w.shape[1]
    q_lora = w_uq.shape[0]
    kv_lora = w_ukv.shape[0]
    heads = w_uq.shape[1]
    ssm_width = d_skip.shape[0]
    o1 = pool_width
    o2 = o1 + q_lora
    o3 = o2 + kv_lora
    o4 = o3 + QK_ROPE
    o5 = o4 + ssm_width

    w_kpe = w_in[:, o3:o4]
    small_cols = [w_in[:, :o1], w_in[:, o1:o2], w_in[:, o4:o5], w_in[:, o2:o3], w_kpe, _rotate_half_cols(w_kpe)]
    n_small = o5 + QK_ROPE
    tn_small = 1280
    n_pad = -n_small % tn_small
    w_small = jnp.concatenate(small_cols + [jnp.zeros((d, n_pad), w_in.dtype)], axis=1).astype(BF16)
    hn = rmsnorm(x, norm_mix, BF16)
    zs = matmul(hn, w_small, F32, tm=512, tn=tn_small, name="in_proj_small")
    gates = matmul(hn, w_in[:, o5:].astype(BF16), BF16, tm=1024, tn=1024, act="sigmoid", name="in_proj_gates")
    cq_block = o1 // q_lora
    ssm_block = o2 // ssm_width
    ckv_block = (o2 + ssm_width) // kv_lora
    kpe_block = (o2 + ssm_width + kv_lora) // LANES
    assert o1 % q_lora == 0 and o2 % ssm_width == 0 and (o2 + ssm_width) % kv_lora == 0

    ya = pool_mixer(zs, pool_w, pool_scale, seqs)

    wq_nope = w_uq[:, :, :QK_NOPE]
    wq_pe = w_uq[:, :, QK_NOPE:]
    wq = jnp.concatenate([wq_nope, wq_pe, _rotate_half_cols(wq_pe)], axis=-1).reshape(q_lora, heads * HEAD_PAD)
    qp = q_proj(zs, cq_block, q_norm, wq.astype(BF16), cs, heads)
    wk = w_ukv[:, :, :QK_NOPE].reshape(kv_lora, heads * QK_NOPE).astype(BF16)
    wvt = w_ukv[:, :, QK_NOPE:].reshape(kv_lora, heads * V_HEAD).T.astype(BF16)
    kp, v = kv_proj(zs, ckv_block, kpe_block, kv_norm, wk, wvt, cs, heads)
    groups = []
    for a, b in seqs:
        if groups and groups[-1][2] == b - a and groups[-1][0] + groups[-1][1] * groups[-1][2] == a:
            groups[-1][1] += 1
        else:
            groups.append([a, 1, b - a])
    yb = None
    for row0, nseq, seq in groups:
        yb = attention(qp, kp, v, row0, nseq, seq, heads, prev=yb)

    u_ssm = lax.slice_in_dim(zs, ssm_block * ssm_width, (ssm_block + 1) * ssm_width, axis=1)
    s5w = _s5_weights(a_re, a_im, log_dt, b_re, b_im, c_re, c_im, d_skip)
    yc = glu(s5_mix(u_ssm, s5w, seqs), w_glu.astype(BF16))

    merged = gated_merge(ya, yb, yc, p_a.astype(BF16), p_b.astype(BF16), p_c.astype(BF16), gates)
    return matmul(merged, w_o.astype(BF16), F32, tm=1024, tn=512, residual=x, name="out_proj")


def _peer(x, layer, norm_ffn, peer_wq, peer_keys, u_tabs, v_tabs):
    h2 = rmsnorm(x, norm_ffn, BF16)
    pheads = peer_keys.shape[0]
    q = matmul(h2, peer_wq.astype(BF16), F32, tm=1024, tn=1024, name="peer_query")
    keys = peer_keys.reshape(pheads * 2, N_KEYS, peer_keys.shape[-1]).astype(F32)
    s1, s2, e2, tau, m1 = peer_select(q, keys, pheads)
    return peer_dense(x, h2, u_tabs, v_tabs, layer, s1, s2, e2, tau, m1, pheads)


def kernel(x_prompt, x_sample, norm_mix, w_in, pool_w, pool_scale, q_norm, w_uq, kv_norm, w_ukv, ssm_a_re, ssm_a_im, ssm_log_dt, ssm_b_re, ssm_b_im, ssm_c_re, ssm_c_im, ssm_d, w_glu, p_a, p_b, p_c, w_o, norm_ffn, peer_wq, peer_keys, peer_u, peer_v, final_norm):
    d = x_prompt.shape[-1]
    seqs = []
    row = 0
    for xs in (x_prompt, x_sample):
        for _ in range(xs.shape[0]):
            seqs.append((row, row + xs.shape[1]))
            row += xs.shape[1]
    x = jnp.concatenate([x_prompt.reshape(-1, d), x_sample.reshape(-1, d)], axis=0)
    cs = _rope_table(seqs)
    per_layer = (norm_mix, w_in, pool_w, pool_scale, q_norm, w_uq, kv_norm, w_ukv, ssm_a_re, ssm_a_im, ssm_log_dt,
                 ssm_b_re, ssm_b_im, ssm_c_re, ssm_c_im, ssm_d, w_glu, p_a, p_b, p_c, w_o, norm_ffn, peer_wq,
                 peer_keys)
    n_mix = 21
    u_tabs, v_tabs = peer_u.astype(BF16), peer_v.astype(BF16)
    for layer in range(norm_mix.shape[0]):
        lw = [w[layer] for w in per_layer]
        x = _mixer(x, seqs, cs, *lw[:n_mix])
        x = _peer(x, layer, *lw[n_mix:], u_tabs, v_tabs)
    n_prompt = x_prompt.shape[0] * x_prompt.shape[1]
    y_prompt = rmsnorm(x, final_norm, x_prompt.dtype, row0=0, nrows=n_prompt)
    y_sample = rmsnorm(x, final_norm, x_sample.dtype, row0=n_prompt, nrows=x.shape[0] - n_prompt)
    return (y_prompt.reshape(x_prompt.shape), y_sample.reshape(x_sample.shape))
```

```python
import functools
import math

import jax
import jax.numpy as jnp
from jax import lax
from jax.experimental import pallas as pl
from jax.experimental.pallas import tpu as pltpu

F32 = jnp.float32
BF16 = jnp.bfloat16
F8 = jnp.float8_e4m3fn
F8_TARGET = 240.0

EPS = 1e-6
POOL_WINDOWS = (2, 4, 8, 16)
QK_NOPE = 128
QK_ROPE = 64
V_HEAD = 128
ROPE_BASE = 10000.0
SSM_GROUP = 16
SSM_STATE = 64
PEER_TOPK = 16
N_KEYS = 128

LANES = 128
SUBLANES = 8
HEAD_PAD = 256
SSM_CHUNK = 16
VMEM_LIMIT = 56 * 1024 * 1024
POOL_HALO = 8


def _cparams(sem, flags=None):
    return pltpu.CompilerParams(dimension_semantics=sem, vmem_limit_bytes=VMEM_LIMIT, flags=flags)


def _tile(pref, dim):
    t = min(pref, dim)
    assert dim % t == 0, (pref, dim)
    return t


def _gelu_tanh(x):
    cdf = 0.5 * (1.0 + jnp.tanh(math.sqrt(2.0 / math.pi) * (x + 0.044715 * (x * x * x))))
    return x * cdf


def _rms(x, g):
    y = x * lax.rsqrt(jnp.mean(x * x, axis=-1, keepdims=True) + EPS)
    return y * g


def _pow2_scale(bound):
    safe = jnp.where(bound > 0, bound, 1.0).astype(F32)
    return jnp.where(bound > 0, jnp.exp2(jnp.floor(jnp.log2(F8_TARGET / safe))), 1.0).astype(F32)


def _to_f8(w):
    s = _pow2_scale(jnp.max(jnp.abs(w)))
    return (w.astype(F32) * s).astype(F8), s


def _rmsnorm_body(x_ref, g_ref, o_ref):
    o_ref[...] = _rms(x_ref[...].astype(F32), g_ref[...]).astype(o_ref.dtype)


def _rmsnorm_f8_body(x_ref, g_ref, s_ref, o_ref, o8_ref):
    y = _rms(x_ref[...].astype(F32), g_ref[...])
    o_ref[...] = y.astype(o_ref.dtype)
    o8_ref[...] = (y * s_ref[...]).astype(o8_ref.dtype)


def rmsnorm(x, g, out_dtype, tm=512, row0=0, nrows=None, with_f8=False):
    d = x.shape[1]
    t = x.shape[0] if nrows is None else nrows
    tm = _tile(tm, math.gcd(t, row0) if row0 else t)
    b0 = row0 // tm
    xspec = pl.BlockSpec((tm, d), lambda i: (b0 + i, 0))
    gspec = pl.BlockSpec((1, d), lambda i: (0, 0))
    ospec = pl.BlockSpec((tm, d), lambda i: (i, 0))
    g2 = g.reshape(1, d).astype(F32)
    if not with_f8:
        return pl.pallas_call(
            _rmsnorm_body,
            grid=(t // tm,),
            in_specs=[xspec, gspec],
            out_specs=ospec,
            out_shape=jax.ShapeDtypeStruct((t, d), out_dtype),
            compiler_params=_cparams(("parallel",)),
            name="rmsnorm",
        )(x, g2)
    s = _pow2_scale(math.sqrt(d) * jnp.max(jnp.abs(g2)))
    y, y8 = pl.pallas_call(
        _rmsnorm_f8_body,
        grid=(t // tm,),
        in_specs=[xspec, gspec, pl.BlockSpec((1, 1), lambda i: (0, 0))],
        out_specs=[ospec, ospec],
        out_shape=[jax.ShapeDtypeStruct((t, d), out_dtype), jax.ShapeDtypeStruct((t, d), F8)],
        compiler_params=_cparams(("parallel",)),
        name="rmsnorm_f8",
    )(x, g2, s.reshape(1, 1))
    return y, y8, s


def _mm_body(a_ref, b_ref, o_ref, *, act):
    acc = jnp.dot(a_ref[...], b_ref[...], preferred_element_type=F32)
    if act == "sigmoid":
        acc = jax.nn.sigmoid(acc)
    o_ref[...] = acc.astype(o_ref.dtype)


def _mm_scaled_body(a_ref, b_ref, s_ref, o_ref, *, act):
    acc = jnp.dot(a_ref[...], b_ref[...], preferred_element_type=F32) * s_ref[...]
    if act == "sigmoid":
        acc = jax.nn.sigmoid(acc)
    o_ref[...] = acc.astype(o_ref.dtype)


def _mm_res_body(a_ref, b_ref, r_ref, o_ref):
    acc = jnp.dot(a_ref[...], b_ref[...], preferred_element_type=F32)
    o_ref[...] = (r_ref[...] + acc).astype(o_ref.dtype)


def matmul(a, b, out_dtype, tm, tn, act=None, residual=None, out_scale=None, name="matmul"):
    m, k = a.shape
    n = b.shape[1]
    tm = _tile(tm, m)
    tn = _tile(tn, n)
    in_specs = [pl.BlockSpec((tm, k), lambda i, j: (i, 0)),
                pl.BlockSpec((k, tn), lambda i, j: (0, j))]
    args = [a, b]
    if out_scale is not None:
        body = functools.partial(_mm_scaled_body, act=act)
        in_specs.append(pl.BlockSpec((1, 1), lambda i, j: (0, 0)))
        args.append(out_scale.reshape(1, 1).astype(F32))
    elif residual is None:
        body = functools.partial(_mm_body, act=act)
    else:
        body = _mm_res_body
        in_specs.append(pl.BlockSpec((tm, tn), lambda i, j: (i, j)))
        args.append(residual)
    return pl.pallas_call(
        body,
        grid=(m // tm, n // tn),
        in_specs=in_specs,
        out_specs=pl.BlockSpec((tm, tn), lambda i, j: (i, j)),
        out_shape=jax.ShapeDtypeStruct((m, n), out_dtype),
        compiler_params=_cparams(("parallel", "parallel")),
        name=name,
    )(*args)


def _seq_bounds(r0, seqs):
    s0 = jnp.int32(0)
    s1 = jnp.int32(0)
    for a, b in seqs:
        inside = jnp.logical_and(r0 >= a, r0 < b)
        s0 = jnp.where(inside, jnp.int32(a), s0)
        s1 = jnp.where(inside, jnp.int32(b), s1)
    return s0, s1


def _pool_body(prev_ref, cur_ref, next_ref, w_ref, sc_ref, o_ref, *, ts, seqs, group):
    r0 = pl.program_id(0) * ts
    s0, s1 = _seq_bounds(r0, seqs)
    cur = cur_ref[...].astype(F32)
    prev = jnp.where(r0 > s0, prev_ref[...].astype(F32), 0.0)
    nxt = jnp.where(r0 + ts < s1, next_ref[...].astype(F32), 0.0)
    ext = jnp.concatenate([prev, cur, nxt], axis=0).astype(BF16)
    te = ts + 2 * POOL_HALO
    d = lax.broadcasted_iota(jnp.int32, (ts, te), 1) - lax.broadcasted_iota(jnp.int32, (ts, te), 0)
    rows = r0 + lax.broadcasted_iota(jnp.int32, (ts, group), 0)
    for gi, w in enumerate(POOL_WINDOWS):
        left = w // 2
        right = w - 1 - left
        band = jnp.where(d >= POOL_HALO - left, jnp.where(d <= POOL_HALO + right, 1.0, 0.0), 0.0).astype(BF16)
        sl = slice(gi * group, (gi + 1) * group)
        tot = jnp.dot(band, ext[:, sl], preferred_element_type=F32)
        lo = jnp.maximum(rows - left, s0)
        hi = jnp.minimum(rows + right, s1 - 1)
        cnt = (hi - lo + 1).astype(F32)
        p = tot / cnt - cur[:, sl]
        y = jnp.dot(p.astype(BF16), w_ref[gi], preferred_element_type=F32) * sc_ref[:, sl]
        o_ref[:, sl] = y.astype(o_ref.dtype)


def pool_mixer(zs, pool_w, pool_scale, seqs, ts=512):
    t = zs.shape[0]
    ng, group, _ = pool_w.shape
    width = ng * group
    ts = _tile(ts, min(b - a for a, b in seqs))
    hb = ts // POOL_HALO
    nrow8 = t // POOL_HALO
    body = functools.partial(_pool_body, ts=ts, seqs=tuple(seqs), group=group)
    return pl.pallas_call(
        body,
        grid=(t // ts,),
        in_specs=[pl.BlockSpec((POOL_HALO, width), lambda i: (jnp.maximum(i * hb - 1, 0), 0)),
                  pl.BlockSpec((ts, width), lambda i: (i, 0)),
                  pl.BlockSpec((POOL_HALO, width), lambda i: (jnp.minimum((i + 1) * hb, nrow8 - 1), 0)),
                  pl.BlockSpec((ng, group, group), lambda i: (0, 0, 0)),
                  pl.BlockSpec((1, width), lambda i: (0, 0))],
        out_specs=pl.BlockSpec((ts, width), lambda i: (i, 0)),
        out_shape=jax.ShapeDtypeStruct((t, width), BF16),
        compiler_params=_cparams(("parallel",)),
        name="pool_mixer",
    )(zs, zs, zs, pool_w.astype(BF16), pool_scale.reshape(1, width).astype(F32))


def _rope_combine(x, cs):
    tmp = x * cs
    return tmp + pltpu.roll(tmp, LANES // 2, axis=1)


def _qproj_body(z_ref, g_ref, w_ref, cs_ref, o_ref, *, heads, scale):
    cn = _rms(z_ref[...].astype(F32), g_ref[...]).astype(BF16)
    acc = jnp.dot(cn, w_ref[...], preferred_element_type=F32)
    cs = cs_ref[...] * scale
    for h in range(heads):
        lo = acc[:, h * HEAD_PAD:h * HEAD_PAD + LANES] * scale
        hi = _rope_combine(acc[:, h * HEAD_PAD + LANES:(h + 1) * HEAD_PAD], cs)
        o_ref[:, h * HEAD_PAD:h * HEAD_PAD + LANES] = lo.astype(o_ref.dtype)
        o_ref[:, h * HEAD_PAD + LANES:(h + 1) * HEAD_PAD] = hi.astype(o_ref.dtype)


def q_proj(zs, col_block, q_norm, wq, cs, heads, tm=512):
    t = zs.shape[0]
    r = wq.shape[0]
    tm = _tile(tm, t)
    scale = (QK_NOPE + QK_ROPE) ** -0.5 * math.log2(math.e)
    body = functools.partial(_qproj_body, heads=heads, scale=scale)
    return pl.pallas_call(
        body,
        grid=(t // tm,),
        in_specs=[pl.BlockSpec((tm, r), lambda i: (i, col_block)),
                  pl.BlockSpec((1, r), lambda i: (0, 0)),
                  pl.BlockSpec((r, heads * HEAD_PAD), lambda i: (0, 0)),
                  pl.BlockSpec((tm, LANES), lambda i: (i, 0))],
        out_specs=pl.BlockSpec((tm, heads * HEAD_PAD), lambda i: (i, 0)),
        out_shape=jax.ShapeDtypeStruct((t, heads * HEAD_PAD), BF16),
        compiler_params=_cparams(("parallel",)),
        name="q_proj",
    )(zs, q_norm.reshape(1, r).astype(F32), wq, cs)


def _kvproj_body(z_ref, g_ref, kpe_ref, wk_ref, wvt_ref, cs_ref, k_ref, vt_ref, *, heads):
    cn = _rms(z_ref[...].astype(F32), g_ref[...]).astype(BF16)
    kn = jnp.dot(cn, wk_ref[...], preferred_element_type=F32)
    vt = lax.dot_general(wvt_ref[...], cn, (((1,), (1,)), ((), ())), preferred_element_type=F32)
    kr = _rope_combine(kpe_ref[...].astype(F32), cs_ref[...])
    lane = lax.broadcasted_iota(jnp.int32, kr.shape, 1)
    kr = jnp.where(lane < QK_ROPE, kr, 0.0).astype(k_ref.dtype)
    for h in range(heads):
        k_ref[:, h * HEAD_PAD:h * HEAD_PAD + LANES] = kn[:, h * LANES:(h + 1) * LANES].astype(k_ref.dtype)
        k_ref[:, h * HEAD_PAD + LANES:(h + 1) * HEAD_PAD] = kr
    vt_ref[...] = vt.astype(vt_ref.dtype)


def kv_proj(zs, ckv_block, kpe_block, kv_norm, wk, wvt, cs, heads, tm=512):
    t = zs.shape[0]
    r = wk.shape[0]
    tm = _tile(tm, t)
    body = functools.partial(_kvproj_body, heads=heads)
    return pl.pallas_call(
        body,
        grid=(t // tm,),
        in_specs=[pl.BlockSpec((tm, r), lambda i: (i, ckv_block)),
                  pl.BlockSpec((1, r), lambda i: (0, 0)),
                  pl.BlockSpec((tm, LANES), lambda i: (i, kpe_block)),
                  pl.BlockSpec((r, heads * QK_NOPE), lambda i: (0, 0)),
                  pl.BlockSpec((heads * V_HEAD, r), lambda i: (0, 0)),
                  pl.BlockSpec((tm, LANES), lambda i: (i, 0))],
        out_specs=[pl.BlockSpec((tm, heads * HEAD_PAD), lambda i: (i, 0)),
                   pl.BlockSpec((heads * V_HEAD, tm), lambda i: (0, i))],
        out_shape=[jax.ShapeDtypeStruct((t, heads * HEAD_PAD), BF16),
                   jax.ShapeDtypeStruct((heads * V_HEAD, t), BF16)],
        compiler_params=_cparams(("parallel",)),
        name="kv_proj",
    )(zs, kv_norm.reshape(1, r).astype(F32), zs, wk, wvt, cs)


def _sublane_allreduce(x, op):
    for shift in (4, 2, 1):
        x = op(x, pltpu.roll(x, shift, axis=0))
    return x


def _attn_body(q_ref, k_ref, vt_ref, o_ref, s_scr, smax_scr, *, tk, nsplit):
    tq = q_ref.shape[0]
    tw = tq // nsplit
    nk = k_ref.shape[0] // tk
    vr = V_HEAD // SUBLANES
    qt = q_ref[...].astype(F32).T.astype(BF16)
    qts = [qt[:, i * tw:(i + 1) * tw] for i in range(nsplit)]

    def produce(j, slot):
        ks = k_ref[pl.ds(pl.multiple_of(j * tk, tk), tk), :]
        for i, qh in enumerate(qts):
            s = jnp.dot(ks, qh, preferred_element_type=F32).reshape(tk // SUBLANES, SUBLANES, tw)
            s_scr[slot, i] = s
            smax_scr[slot, i] = jnp.max(s, axis=0)

    def consume(j, slot, state):
        vts = vt_ref[:, pl.ds(pl.multiple_of(j * tk, tk), tk)]
        new = []
        for i, (m, l, acc) in enumerate(state):
            mn = jnp.maximum(m, _sublane_allreduce(smax_scr[slot, i], jnp.maximum))
            p = jnp.exp2(s_scr[slot, i] - mn[None])
            alpha = jnp.exp2(m - mn)
            l = alpha * l + jnp.sum(p, axis=0)
            pv = jnp.dot(vts, p.reshape(tk, tw).astype(BF16), preferred_element_type=F32)
            acc = alpha[None] * acc + pv.reshape(vr, SUBLANES, tw)
            new.append((mn, l, acc))
        return tuple(new)

    def pair(jj, state):
        j = 2 * jj
        produce(j + 1, 1)
        state = consume(j, 0, state)
        produce(j + 2, 0)
        return consume(j + 1, 1, state)

    state = tuple((jnp.full((SUBLANES, tw), -jnp.inf, F32), jnp.zeros((SUBLANES, tw), F32),
                   jnp.zeros((vr, SUBLANES, tw), F32)) for _ in range(nsplit))
    produce(0, 0)
    if nk == 1:
        fin = consume(0, 0, state)
    else:
        assert nk % 2 == 0
        state = lax.fori_loop(0, nk // 2 - 1, pair, state)
        produce(nk - 1, 1)
        fin = consume(nk - 1, 1, consume(nk - 2, 0, state))
    for i, (_, l, acc) in enumerate(fin):
        lt = _sublane_allreduce(l, jnp.add)
        o = (acc / lt[None]).reshape(V_HEAD, tw)
        o_ref[i * tw:(i + 1) * tw, :] = o.T.astype(o_ref.dtype)


def _attn_alias_body(q_ref, k_ref, vt_ref, prev_ref, o_ref, s_scr, smax_scr, *, tk, nsplit):
    del prev_ref
    _attn_body(q_ref, k_ref, vt_ref, o_ref, s_scr, smax_scr, tk=tk, nsplit=nsplit)


def attention(qp, kp, vt, row0, nseq, seq, heads, prev=None, tq=512, tk=512, nsplit=2):
    t = qp.shape[0]
    tq = _tile(tq, seq)
    tk = _tile(tk, seq)
    nq = seq // tq
    q0 = row0 // tq
    b0 = row0 // seq
    assert row0 % seq == 0
    in_specs = [pl.BlockSpec((tq, HEAD_PAD), lambda b, h, i: (q0 + b * nq + i, h)),
                pl.BlockSpec((seq, HEAD_PAD), lambda b, h, i: (b0 + b, h)),
                pl.BlockSpec((V_HEAD, seq), lambda b, h, i: (h, b0 + b))]
    args = [qp, kp, vt]
    if prev is None:
        body, aliases = _attn_body, {}
    else:
        body, aliases = _attn_alias_body, {3: 0}
        in_specs.append(pl.BlockSpec(memory_space=pl.ANY))
        args.append(prev)
    return pl.pallas_call(
        functools.partial(body, tk=tk, nsplit=nsplit),
        grid=(nseq, heads, nq),
        in_specs=in_specs,
        out_specs=pl.BlockSpec((tq, V_HEAD), lambda b, h, i: (q0 + b * nq + i, h)),
        out_shape=jax.ShapeDtypeStruct((t, heads * V_HEAD), BF16),
        scratch_shapes=[pltpu.VMEM((2, nsplit, tk // SUBLANES, SUBLANES, tq // nsplit), F32),
                        pltpu.VMEM((2, nsplit, SUBLANES, tq // nsplit), F32)],
        input_output_aliases=aliases,
        compiler_params=_cparams(("parallel", "parallel", "parallel")),
        name="mla_attention",
    )(*args)


def _s5_weights(a_re, a_im, log_dt, b_re, b_im, c_re, c_im, d_skip):
    L = SSM_CHUNK
    lam = lax.complex(a_re.astype(F32), a_im.astype(F32))
    dt = jnp.exp(log_dt.astype(F32))[..., None]
    z = lam * dt
    lam_bar = jnp.exp(z)
    ks = jnp.arange(L + 1, dtype=F32)
    pw = jnp.exp(z[..., None] * ks.astype(jnp.complex64))
    b_bar = ((lam_bar - 1.0) / lam)[..., None] * lax.complex(b_re.astype(F32), b_im.astype(F32))
    c_mat = lax.complex(c_re.astype(F32), c_im.astype(F32))
    g, n, p = b_bar.shape[1:]
    kern = jnp.einsum("dgpn,dgnk,dgnq->dgkpq", c_mat, pw[..., :L], b_bar).real
    li = jnp.arange(L)
    diff = li[None, :] - li[:, None]
    kf = jnp.take(kern[0], jnp.clip(diff, 0, L - 1), axis=1)
    kb = jnp.take(kern[1], jnp.clip(-diff, 0, L - 1), axis=1)
    dm = diff[None, :, :, None, None]
    eye = jnp.eye(p, dtype=F32) * d_skip.astype(F32).reshape(g, 1, 1, p, 1)
    m = jnp.where(dm > 0, kf, 0.0) + jnp.where(dm < 0, kb, 0.0) + jnp.where(dm == 0, kf + kb + eye, 0.0)
    m = m.transpose(0, 1, 4, 2, 3).reshape(g, L * p, L * p)

    cf = jnp.einsum("gnl,gnq->glqn", pw[0][..., L - 1 - li], b_bar[0])
    cb = jnp.einsum("gnl,gnq->glqn", pw[1][..., li], b_bar[1])
    cf = cf.reshape(g, L * p, n)
    cb = cb.reshape(g, L * p, n)
    bx = jnp.concatenate([cf.real, cf.imag, cf.imag, cf.real, cb.real, cb.imag, cb.imag, cb.real], axis=-1)

    wf = jnp.einsum("gpn,gnl->gnlp", c_mat[0], pw[0][..., li + 1]).reshape(g, n, L * p)
    wb = jnp.einsum("gpn,gnl->gnlp", c_mat[1], pw[1][..., L - li]).reshape(g, n, L * p)
    cyf = jnp.concatenate([wf.real, -wf.imag], axis=1)
    cyb = jnp.concatenate([wb.real, -wb.imag], axis=1)

    al = pw[..., L]

    def pack(x, y):
        return jnp.concatenate([x, y], axis=-1).reshape(1, g * 2 * n)

    a1f, a2f = pack(al[0].real, al[0].real), pack(-al[0].imag, al[0].imag)
    a1b, a2b = pack(al[1].real, al[1].real), pack(-al[1].imag, al[1].imag)
    return (m.astype(BF16), bx.astype(BF16), cyf.astype(BF16), cyb.astype(BF16),
            a1f.astype(F32), a2f.astype(F32), a1b.astype(F32), a2b.astype(F32))


def _s5x_body(u_ref, bx_ref, xf_ref, xfs_ref, xb_ref, xbs_ref):
    x = jnp.dot(u_ref[0], bx_ref[0], preferred_element_type=F32)
    xf_ref[...] = x[:, 0 * LANES:1 * LANES]
    xfs_ref[...] = x[:, 1 * LANES:2 * LANES]
    xb_ref[...] = x[:, 2 * LANES:3 * LANES]
    xbs_ref[...] = x[:, 3 * LANES:4 * LANES]


def _s5scan_body(xf_ref, xfs_ref, xb_ref, xbs_ref, a1f_ref, a2f_ref, a1b_ref, a2b_ref, hf_ref, hb_ref, *,
                 seq_chunks):
    a1f, a2f, a1b, a2b = a1f_ref[...], a2f_ref[...], a1b_ref[...], a2b_ref[...]
    zero = jnp.zeros_like(a1f)
    for c0, n in seq_chunks:

        def step(i, carry, c0=c0, n=n):
            hf, hfs, hb, hbs = carry
            rf = c0 + i
            rb = c0 + n - 1 - i
            hf_ref[pl.ds(rf, 1), :] = hf
            hb_ref[pl.ds(rb, 1), :] = hb
            xf = xf_ref[pl.ds(rf, 1), :]
            xfs = xfs_ref[pl.ds(rf, 1), :]
            xb = xb_ref[pl.ds(rb, 1), :]
            xbs = xbs_ref[pl.ds(rb, 1), :]
            nhf = a1f * hf + a2f * hfs + xf
            nhfs = a1f * hfs - a2f * hf + xfs
            nhb = a1b * hb + a2b * hbs + xb
            nhbs = a1b * hbs - a2b * hb + xbs
            return nhf, nhfs, nhb, nhbs

        lax.fori_loop(0, n, step, (zero, zero, zero, zero))


def _s5y_body(u_ref, m_ref, hf_ref, hb_ref, cyf_ref, cyb_ref, y_ref):
    y = jnp.dot(u_ref[0], m_ref[0], preferred_element_type=F32)
    y = y + jnp.dot(hf_ref[...].astype(BF16), cyf_ref[0], preferred_element_type=F32)
    y = y + jnp.dot(hb_ref[...].astype(BF16), cyb_ref[0], preferred_element_type=F32)
    y_ref[0] = y


def s5_mix(u, weights, seqs):
    m, bx, cyf, cyb, a1f, a2f, a1b, a2b = weights
    t = u.shape[0]
    g = m.shape[0]
    L, p = SSM_CHUNK, SSM_GROUP
    lp = L * p
    c = t // L
    ug = u.reshape(c, L, g, p).transpose(2, 0, 1, 3).reshape(g, c, lp).astype(BF16)
    xs = pl.pallas_call(
        _s5x_body,
        grid=(g,),
        in_specs=[pl.BlockSpec((1, c, lp), lambda i: (i, 0, 0)),
                  pl.BlockSpec((1, lp, 4 * LANES), lambda i: (i, 0, 0))],
        out_specs=[pl.BlockSpec((c, LANES), lambda i: (0, i))] * 4,
        out_shape=[jax.ShapeDtypeStruct((c, g * LANES), F32)] * 4,
        compiler_params=_cparams(("parallel",)),
        name="s5_chunk_states",
    )(ug, bx)
    seq_chunks = tuple((a // L, (b - a) // L) for a, b in seqs)
    tc = _tile(512, g * LANES)
    xspec = pl.BlockSpec((c, tc), lambda i: (0, i))
    aspec = pl.BlockSpec((1, tc), lambda i: (0, i))
    hf, hb = pl.pallas_call(
        functools.partial(_s5scan_body, seq_chunks=seq_chunks),
        grid=(g * LANES // tc,),
        in_specs=[xspec] * 4 + [aspec] * 4,
        out_specs=[xspec] * 2,
        out_shape=[jax.ShapeDtypeStruct((c, g * LANES), F32)] * 2,
        compiler_params=_cparams(("parallel",)),
        name="s5_state_scan",
    )(*xs, a1f, a2f, a1b, a2b)
    yg = pl.pallas_call(
        _s5y_body,
        grid=(g,),
        in_specs=[pl.BlockSpec((1, c, lp), lambda i: (i, 0, 0)),
                  pl.BlockSpec((1, lp, lp), lambda i: (i, 0, 0)),
                  pl.BlockSpec((c, LANES), lambda i: (0, i)),
                  pl.BlockSpec((c, LANES), lambda i: (0, i)),
                  pl.BlockSpec((1, LANES, lp), lambda i: (i, 0, 0)),
                  pl.BlockSpec((1, LANES, lp), lambda i: (i, 0, 0))],
        out_specs=pl.BlockSpec((1, c, lp), lambda i: (i, 0, 0)),
        out_shape=jax.ShapeDtypeStruct((g, c, lp), F32),
        compiler_params=_cparams(("parallel",)),
        name="s5_outputs",
    )(ug, m, hf, hb, cyf, cyb)
    return yg.reshape(g, c, L, p).transpose(1, 2, 0, 3).reshape(t, g * p)


def _glu_body(y_ref, w_ref, o_ref):
    gl = _gelu_tanh(y_ref[...])
    zz = jnp.dot(gl.astype(BF16), w_ref[...], preferred_element_type=F32)
    o_ref[...] = (gl * jax.nn.sigmoid(zz)).astype(o_ref.dtype)


def glu(y, w_glu, tm=512):
    t, w = y.shape
    tm = _tile(tm, t)
    return pl.pallas_call(
        _glu_body,
        grid=(t // tm,),
        in_specs=[pl.BlockSpec((tm, w), lambda i: (i, 0)),
                  pl.BlockSpec((w, w), lambda i: (0, 0))],
        out_specs=pl.BlockSpec((tm, w), lambda i: (i, 0)),
        out_shape=jax.ShapeDtypeStruct((t, w), BF16),
        compiler_params=_cparams(("parallel",)),
        name="s5_glu",
    )(y, w_glu)


def _merge_body(ya_ref, yb_ref, yc_ref, pa_ref, pb_ref, pc_ref, ga_ref, gb_ref, gc_ref, o_ref):
    a = jnp.dot(ya_ref[...], pa_ref[...], preferred_element_type=F32)
    b = jnp.dot(yb_ref[...], pb_ref[...], preferred_element_type=F32)
    c = jnp.dot(yc_ref[...], pc_ref[...], preferred_element_type=F32)
    o = ga_ref[...].astype(F32) * a + gb_ref[...].astype(F32) * b + gc_ref[...].astype(F32) * c
    o_ref[...] = o.astype(o_ref.dtype)


def gated_merge(ya, yb, yc, pa, pb, pc, gates, tm=512, tn=1024):
    t = ya.shape[0]
    d = pa.shape[1]
    tm = _tile(tm, t)
    tn = _tile(tn, d)
    nj = d // tn

    def aspec(x):
        return pl.BlockSpec((tm, x.shape[1]), lambda i, j: (i, 0))

    def wspec(x):
        return pl.BlockSpec((x.shape[0], tn), lambda i, j: (0, j))

    gspecs = [pl.BlockSpec((tm, tn), functools.partial(lambda i, j, k: (i, k * nj + j), k=k)) for k in range(3)]
    return pl.pallas_call(
        _merge_body,
        grid=(t // tm, nj),
        in_specs=[aspec(ya), aspec(yb), aspec(yc), wspec(pa), wspec(pb), wspec(pc)] + gspecs,
        out_specs=pl.BlockSpec((tm, tn), lambda i, j: (i, j)),
        out_shape=jax.ShapeDtypeStruct((t, d), BF16),
        compiler_params=_cparams(("parallel", "parallel")),
        name="gated_merge",
    )(ya, yb, yc, pa, pb, pc, gates, gates, gates)


def _oddeven_merge(lo, hi, r):
    step = r * 2
    if step < hi - lo:
        yield from _oddeven_merge(lo, hi, step)
        yield from _oddeven_merge(lo + r, hi, step)
        yield from [(i, i + r) for i in range(lo + r, hi - r, step)]
    else:
        yield (lo, lo + r)


def _oddeven_merge_sort(lo, hi):
    if hi - lo >= 1:
        mid = lo + (hi - lo) // 2
        yield from _oddeven_merge_sort(lo, mid)
        yield from _oddeven_merge_sort(mid + 1, hi)
        yield from _oddeven_merge(lo, hi, 1)


_SORT16 = tuple(_oddeven_merge_sort(0, PEER_TOPK - 1))
_BITONIC16 = tuple((i, i + d) for d in (8, 4, 2, 1) for i in range(PEER_TOPK) if not i & d)


def _compare_exchange(v, pairs):
    for i, j in pairs:
        hi = jnp.maximum(v[i], v[j])
        lo = jnp.minimum(v[i], v[j])
        v[i], v[j] = hi, lo


def _top16_desc(v):
    v = list(v)
    _compare_exchange(v, _SORT16)
    for shift in (4, 2, 1):
        r = [pltpu.roll(x, shift, axis=0) for x in v]
        v = [jnp.maximum(v[i], r[PEER_TOPK - 1 - i]) for i in range(PEER_TOPK)]
        _compare_exchange(v, _BITONIC16)
    return v


def _peer_select_body(q_ref, keys_ref, s1_ref, s2_ref, e2_ref, tau_ref, m1_ref, *, heads):
    tt = q_ref.shape[0]
    sub = lax.broadcasted_iota(jnp.int32, (SUBLANES, tt), 0)
    ninf = jnp.full((SUBLANES, tt), -jnp.inf, F32)

    def by_sublane(lst):
        out = lst[SUBLANES - 1]
        for j in range(SUBLANES - 2, -1, -1):
            out = jnp.where(sub == j, lst[j], out)
        return out

    for h in range(heads):
        scores, tops = [], []
        for c in range(2):
            blk = 2 * h + c
            qb = q_ref[:, blk * N_KEYS:(blk + 1) * N_KEYS]
            kb = keys_ref[blk]
            q_hi, k_hi = qb.astype(BF16), kb.astype(BF16)
            q_lo = (qb - q_hi.astype(F32)).astype(BF16)
            k_lo = (kb - k_hi.astype(F32)).astype(BF16)
            nt = (((1,), (1,)), ((), ()))
            st = (lax.dot_general(k_hi, q_hi, nt, preferred_element_type=F32)
                  + lax.dot_general(k_hi, q_lo, nt, preferred_element_type=F32)
                  + lax.dot_general(k_lo, q_hi, nt, preferred_element_type=F32))
            scores.append(st)
            tops.append(_top16_desc([st[SUBLANES * a:SUBLANES * (a + 1), :] for a in range(N_KEYS // SUBLANES)]))
        t1, t2 = tops
        t2a, t2b, t1b = by_sublane(t2[:8]), by_sublane(t2[8:]), by_sublane(t1[8:])
        cand = [t1[0] + t2a, t1[0] + t2b, t1[1] + t2a,
                jnp.where(sub < 5, t1[2] + t2a, ninf), jnp.where(sub < 4, t1[3] + t2a, ninf),
                jnp.where(sub < 3, t1[4] + t2a, ninf), jnp.where(sub < 2, t1[5] + t2a, ninf),
                jnp.where(sub < 2, t1[6] + t2a, ninf), jnp.where(sub < 2, t1[7] + t2a, ninf),
                t1b + t2[0]] + [ninf] * 6
        best = _top16_desc(cand)
        z = jnp.exp(best[0] - best[0])
        for b in best[1:]:
            z = z + jnp.exp(b - best[0])
        s1_ref[h] = scores[0]
        s2_ref[h] = scores[1]
        e2_ref[h] = jnp.exp(scores[1] - t2[0][0:1, :]) / z[0:1, :]
        tau_ref[h:h + 1, :] = best[PEER_TOPK - 1][0:1, :]
        m1_ref[h:h + 1, :] = t1[0][0:1, :]


def peer_select(q, keys, heads, tt=256):
    t = q.shape[0]
    tt = _tile(tt, t)
    big = jax.ShapeDtypeStruct((heads, N_KEYS, t), F32)
    small = jax.ShapeDtypeStruct((heads, t), F32)
    bspec = pl.BlockSpec((heads, N_KEYS, tt), lambda i: (0, 0, i))
    sspec = pl.BlockSpec((heads, tt), lambda i: (0, i))
    return pl.pallas_call(
        functools.partial(_peer_select_body, heads=heads),
        grid=(t // tt,),
        in_specs=[pl.BlockSpec((tt, q.shape[1]), lambda i: (i, 0)),
                  pl.BlockSpec(keys.shape, lambda i: (0, 0, 0))],
        out_specs=[bspec, bspec, bspec, sspec, sspec],
        out_shape=[big, big, big, small, small],
        compiler_params=_cparams(("parallel",)),
        name="peer_select",
    )(q, keys)


def _peer_dense_body(x_ref, h_ref, u_ref, v_ref, sc_ref, s1_ref, s2_ref, e2_ref, tau_ref, m1_ref, o_ref, *, heads):
    j = pl.program_id(1)
    te = u_ref.shape[0]
    na = te // N_KEYS

    @pl.when(j == 0)
    def _():
        o_ref[...] = x_ref[...]

    act = lax.dot_general(u_ref[...], h_ref[...], (((1,), (1,)), ((), ())), preferred_element_type=F32)
    act = act * sc_ref[:, 0:1]
    rows = []
    for al in range(na):
        a = j * na + al
        g = None
        for h in range(heads):
            s1row = s1_ref[h, pl.ds(a, 1), :]
            e1row = jnp.exp(s1row - m1_ref[h:h + 1, :])
            sel = (s1row + s2_ref[h]) >= tau_ref[h:h + 1, :]
            contrib = jnp.where(sel, e1row * e2_ref[h], 0.0)
            g = contrib if g is None else g + contrib
        rows.append(g)
    gate = rows[0] if na == 1 else jnp.concatenate(rows, axis=0)
    wgt = gate * _gelu_tanh(act)
    peak = jnp.max(jnp.abs(wgt), axis=(0, 1), keepdims=True)
    ws = jnp.where(peak > 0.0, jnp.exp2(jnp.floor(jnp.log2(F8_TARGET / jnp.where(peak > 0.0, peak, 1.0)))), 1.0)
    part = lax.dot_general((wgt * ws).astype(F8), v_ref[...], (((0,), (0,)), ((), ())),
                           preferred_element_type=F32)
    o_ref[...] += part * (sc_ref[:, 1:2] / ws)


def peer_dense(x, h, u_tab, v_tab, layer, scales, s1, s2, e2, tau, m1, heads, tm=512, te=512):
    t, d = h.shape
    e = u_tab.shape[1]
    tm = _tile(tm, t)
    te = _tile(te, e)
    once = pl.Buffered(1)
    bspec = pl.BlockSpec((heads, N_KEYS, tm), lambda i, j: (0, 0, i), pipeline_mode=once)
    sspec = pl.BlockSpec((heads, tm), lambda i, j: (0, i))
    return pl.pallas_call(
        functools.partial(_peer_dense_body, heads=heads),
        grid=(t // tm, e // te),
        in_specs=[pl.BlockSpec((tm, d), lambda i, j: (i, 0), pipeline_mode=once),
                  pl.BlockSpec((tm, d), lambda i, j: (i, 0), pipeline_mode=once),
                  pl.BlockSpec((None, te, d), lambda i, j: (layer, j, 0)),
                  pl.BlockSpec((None, te, d), lambda i, j: (layer, j, 0)),
                  pl.BlockSpec((1, 2), lambda i, j: (0, 0)),
                  bspec, bspec, bspec, sspec, sspec],
        out_specs=pl.BlockSpec((tm, d), lambda i, j: (i, 0)),
        out_shape=jax.ShapeDtypeStruct((t, d), F32),
        compiler_params=_cparams(("parallel", "arbitrary")),
        name="peer_dense",
    )(x, h, u_tab, v_tab, scales.astype(F32), s1, s2, e2, tau, m1)


def _rope_table(seqs):
    inv = 1.0 / (ROPE_BASE ** (jnp.arange(0, QK_ROPE, 2, dtype=F32) / QK_ROPE))
    pos = jnp.concatenate([jnp.arange(b - a, dtype=F32) for a, b in seqs])
    ang = pos[:, None] * inv[None, :]
    c, s = jnp.cos(ang), jnp.sin(ang)
    return jnp.concatenate([c, c, s, s], axis=-1)


def _rotate_half_cols(w):
    half = w.shape[-1] // 2
    return jnp.concatenate([-w[..., half:], w[..., :half]], axis=-1)


def _mixer(x, seqs, cs, norm_mix, w_in, pool_w, pool_scale, q_norm, w_uq, kv_norm, w_ukv,
           a_re, a_im, log_dt, b_re, b_im, c_re, c_im, d_skip, w_glu, p_a, p_b, p_c, w_o):
    t, d = x.shape
    pool_width = pool_w.shape[0] * pool_w.shape[1]
    q_lora = w_uq.shape[0]
    kv_lora = w_ukv.shape[0]
    heads = w_uq.shape[1]
    ssm_width = d_skip.shape[0]
    o1 = pool_width
    o2 = o1 + q_lora
    o3 = o2 + kv_lora
    o4 = o3 + QK_ROPE
    o5 = o4 + ssm_width

    w_kpe = w_in[:, o3:o4]
    small_cols = [w_in[:, :o1], w_in[:, o1:o2], w_in[:, o4:o5], w_in[:, o2:o3], w_kpe, _rotate_half_cols(w_kpe)]
    n_small = o5 + QK_ROPE
    tn_small = 1280
    n_pad = -n_small % tn_small
    w_small = jnp.concatenate(small_cols + [jnp.zeros((d, n_pad), w_in.dtype)], axis=1).astype(BF16)
    hn, hn8, s_h = rmsnorm(x, norm_mix, BF16, with_f8=True)
    zs = matmul(hn, w_small, F32, tm=512, tn=tn_small, name="in_proj_small")
    wg8, s_w = _to_f8(w_in[:, o5:])
    gates = matmul(hn8, wg8, BF16, tm=1024, tn=1024, act="sigmoid", out_scale=1.0 / (s_h * s_w),
                   name="in_proj_gates")
    cq_block = o1 // q_lora
    ssm_block = o2 // ssm_width
    ckv_block = (o2 + ssm_width) // kv_lora
    kpe_block = (o2 + ssm_width + kv_lora) // LANES
    assert o1 % q_lora == 0 and o2 % ssm_width == 0 and (o2 + ssm_width) % kv_lora == 0

    ya = pool_mixer(zs, pool_w, pool_scale, seqs)

    wq_nope = w_uq[:, :, :QK_NOPE]
    wq_pe = w_uq[:, :, QK_NOPE:]
    wq = jnp.concatenate([wq_nope, wq_pe, _rotate_half_cols(wq_pe)], axis=-1).reshape(q_lora, heads * HEAD_PAD)
    qp = q_proj(zs, cq_block, q_norm, wq.astype(BF16), cs, heads)
    wk = w_ukv[:, :, :QK_NOPE].reshape(kv_lora, heads * QK_NOPE).astype(BF16)
    wvt = w_ukv[:, :, QK_NOPE:].reshape(kv_lora, heads * V_HEAD).T.astype(BF16)
    kp, v = kv_proj(zs, ckv_block, kpe_block, kv_norm, wk, wvt, cs, heads)
    groups = []
    for a, b in seqs:
        if groups and groups[-1][2] == b - a and groups[-1][0] + groups[-1][1] * groups[-1][2] == a:
            groups[-1][1] += 1
        else:
            groups.append([a, 1, b - a])
    yb = None
    for row0, nseq, seq in groups:
        yb = attention(qp, kp, v, row0, nseq, seq, heads, prev=yb)

    u_ssm = lax.slice_in_dim(zs, ssm_block * ssm_width, (ssm_block + 1) * ssm_width, axis=1)
    s5w = _s5_weights(a_re, a_im, log_dt, b_re, b_im, c_re, c_im, d_skip)
    yc = glu(s5_mix(u_ssm, s5w, seqs), w_glu.astype(BF16))

    merged = gated_merge(ya, yb, yc, p_a.astype(BF16), p_b.astype(BF16), p_c.astype(BF16), gates)
    return matmul(merged, w_o.astype(BF16), F32, tm=1024, tn=512, residual=x, name="out_proj")


def _peer(x, layer, norm_ffn, peer_wq, peer_keys, u_tabs, u_scales, v_tabs, v_scales):
    h2, h8, s_h = rmsnorm(x, norm_ffn, BF16, with_f8=True)
    pheads = peer_keys.shape[0]
    q = matmul(h2, peer_wq.astype(BF16), F32, tm=1024, tn=1024, name="peer_query")
    keys = peer_keys.reshape(pheads * 2, N_KEYS, peer_keys.shape[-1]).astype(F32)
    s1, s2, e2, tau, m1 = peer_select(q, keys, pheads)
    scales = jnp.stack([1.0 / (s_h * u_scales[layer]), 1.0 / v_scales[layer]]).reshape(1, 2)
    return peer_dense(x, h8, u_tabs, v_tabs, layer, scales, s1, s2, e2, tau, m1, pheads)


def kernel(x_prompt, x_sample, norm_mix, w_in, pool_w, pool_scale, q_norm, w_uq, kv_norm, w_ukv, ssm_a_re, ssm_a_im, ssm_log_dt, ssm_b_re, ssm_b_im, ssm_c_re, ssm_c_im, ssm_d, w_glu, p_a, p_b, p_c, w_o, norm_ffn, peer_wq, peer_keys, peer_u, peer_v, final_norm):
    d = x_prompt.shape[-1]
    seqs = []
    row = 0
    for xs in (x_prompt, x_sample):
        for _ in range(xs.shape[0]):
            seqs.append((row, row + xs.shape[1]))
            row += xs.shape[1]
    x = jnp.concatenate([x_prompt.reshape(-1, d), x_sample.reshape(-1, d)], axis=0)
    cs = _rope_table(seqs)
    per_layer = (norm_mix, w_in, pool_w, pool_scale, q_norm, w_uq, kv_norm, w_ukv, ssm_a_re, ssm_a_im, ssm_log_dt,
                 ssm_b_re, ssm_b_im, ssm_c_re, ssm_c_im, ssm_d, w_glu, p_a, p_b, p_c, w_o, norm_ffn, peer_wq,
                 peer_keys)
    n_mix = 21
    u_scales = _pow2_scale(jnp.max(jnp.abs(peer_u), axis=(1, 2)))
    v_scales = _pow2_scale(jnp.max(jnp.abs(peer_v), axis=(1, 2)))
    u_tabs = (peer_u * u_scales[:, None, None]).astype(F8)
    v_tabs = (peer_v * v_scales[:, None, None]).astype(F8)
    for layer in range(norm_mix.shape[0]):
        lw = [w[layer] for w in per_layer]
        x = _mixer(x, seqs, cs, *lw[:n_mix])
        x = _peer(x, layer, *lw[n_mix:], u_tabs, u_scales, v_tabs, v_scales)
    n_prompt = x_prompt.shape[0] * x_prompt.shape[1]
    y_prompt = rmsnorm(x, final_norm, x_prompt.dtype, row0=0, nrows=n_prompt)
    y_sample = rmsnorm(x, final_norm, x_sample.dtype, row0=n_prompt, nrows=x.shape[0] - n_prompt)
    return (y_prompt.reshape(x_prompt.shape), y_sample.reshape(x_sample.shape))
```

```python
import functools
import math

import jax
import jax.numpy as jnp
from jax import lax
from jax.experimental import pallas as pl
from jax.experimental.pallas import tpu as pltpu

F32 = jnp.float32
BF16 = jnp.bfloat16
F8 = jnp.float8_e4m3fn
F8_TARGET = 240.0

EPS = 1e-6
POOL_WINDOWS = (2, 4, 8, 16)
QK_NOPE = 128
QK_ROPE = 64
V_HEAD = 128
ROPE_BASE = 10000.0
SSM_GROUP = 16
SSM_STATE = 64
PEER_TOPK = 16
N_KEYS = 128

LANES = 128
SUBLANES = 8
HEAD_PAD = 256
SSM_CHUNK = 16
VMEM_LIMIT = 56 * 1024 * 1024
POOL_HALO = 8


def _cparams(sem, flags=None):
    return pltpu.CompilerParams(dimension_semantics=sem, vmem_limit_bytes=VMEM_LIMIT, flags=flags)


def _tile(pref, dim):
    t = min(pref, dim)
    assert dim % t == 0, (pref, dim)
    return t


def _gelu_tanh(x):
    cdf = 0.5 * (1.0 + jnp.tanh(math.sqrt(2.0 / math.pi) * (x + 0.044715 * (x * x * x))))
    return x * cdf


def _rms(x, g):
    y = x * lax.rsqrt(jnp.mean(x * x, axis=-1, keepdims=True) + EPS)
    return y * g


def _pow2_scale(bound):
    safe = jnp.where(bound > 0, bound, 1.0).astype(F32)
    return jnp.where(bound > 0, jnp.exp2(jnp.floor(jnp.log2(F8_TARGET / safe))), 1.0).astype(F32)


def _to_f8(w):
    s = _pow2_scale(jnp.max(jnp.abs(w)))
    return (w.astype(F32) * s).astype(F8), s


def _rmsnorm_body(x_ref, g_ref, o_ref):
    o_ref[...] = _rms(x_ref[...].astype(F32), g_ref[...]).astype(o_ref.dtype)


def _rmsnorm_f8_body(x_ref, g_ref, s_ref, o_ref, o8_ref):
    y = _rms(x_ref[...].astype(F32), g_ref[...])
    o_ref[...] = y.astype(o_ref.dtype)
    o8_ref[...] = (y * s_ref[...]).astype(o8_ref.dtype)


def rmsnorm(x, g, out_dtype, tm=512, row0=0, nrows=None, with_f8=False):
    d = x.shape[1]
    t = x.shape[0] if nrows is None else nrows
    tm = _tile(tm, math.gcd(t, row0) if row0 else t)
    b0 = row0 // tm
    xspec = pl.BlockSpec((tm, d), lambda i: (b0 + i, 0))
    gspec = pl.BlockSpec((1, d), lambda i: (0, 0))
    ospec = pl.BlockSpec((tm, d), lambda i: (i, 0))
    g2 = g.reshape(1, d).astype(F32)
    if not with_f8:
        return pl.pallas_call(
            _rmsnorm_body,
            grid=(t // tm,),
            in_specs=[xspec, gspec],
            out_specs=ospec,
            out_shape=jax.ShapeDtypeStruct((t, d), out_dtype),
            compiler_params=_cparams(("parallel",)),
            name="rmsnorm",
        )(x, g2)
    s = _pow2_scale(math.sqrt(d) * jnp.max(jnp.abs(g2)))
    y, y8 = pl.pallas_call(
        _rmsnorm_f8_body,
        grid=(t // tm,),
        in_specs=[xspec, gspec, pl.BlockSpec((1, 1), lambda i: (0, 0))],
        out_specs=[ospec, ospec],
        out_shape=[jax.ShapeDtypeStruct((t, d), out_dtype), jax.ShapeDtypeStruct((t, d), F8)],
        compiler_params=_cparams(("parallel",)),
        name="rmsnorm_f8",
    )(x, g2, s.reshape(1, 1))
    return y, y8, s


def _mm_body(a_ref, b_ref, o_ref, *, act):
    acc = jnp.dot(a_ref[...], b_ref[...], preferred_element_type=F32)
    if act == "sigmoid":
        acc = jax.nn.sigmoid(acc)
    o_ref[...] = acc.astype(o_ref.dtype)


def _mm_scaled_body(a_ref, b_ref, s_ref, o_ref, *, act):
    acc = jnp.dot(a_ref[...], b_ref[...], preferred_element_type=F32) * s_ref[...]
    if act == "sigmoid":
        acc = jax.nn.sigmoid(acc)
    o_ref[...] = acc.astype(o_ref.dtype)


def _mm_res_body(a_ref, b_ref, r_ref, o_ref):
    acc = jnp.dot(a_ref[...], b_ref[...], preferred_element_type=F32)
    o_ref[...] = (r_ref[...] + acc).astype(o_ref.dtype)


def matmul(a, b, out_dtype, tm, tn, act=None, residual=None, out_scale=None, name="matmul"):
    m, k = a.shape
    n = b.shape[1]
    tm = _tile(tm, m)
    tn = _tile(tn, n)
    in_specs = [pl.BlockSpec((tm, k), lambda i, j: (i, 0)),
                pl.BlockSpec((k, tn), lambda i, j: (0, j))]
    args = [a, b]
    if out_scale is not None:
        body = functools.partial(_mm_scaled_body, act=act)
        in_specs.append(pl.BlockSpec((1, 1), lambda i, j: (0, 0)))
        args.append(out_scale.reshape(1, 1).astype(F32))
    elif residual is None:
        body = functools.partial(_mm_body, act=act)
    else:
        body = _mm_res_body
        in_specs.append(pl.BlockSpec((tm, tn), lambda i, j: (i, j)))
        args.append(residual)
    return pl.pallas_call(
        body,
        grid=(m // tm, n // tn),
        in_specs=in_specs,
        out_specs=pl.BlockSpec((tm, tn), lambda i, j: (i, j)),
        out_shape=jax.ShapeDtypeStruct((m, n), out_dtype),
        compiler_params=_cparams(("parallel", "parallel")),
        name=name,
    )(*args)


def _seq_bounds(r0, seqs):
    s0 = jnp.int32(0)
    s1 = jnp.int32(0)
    for a, b in seqs:
        inside = jnp.logical_and(r0 >= a, r0 < b)
        s0 = jnp.where(inside, jnp.int32(a), s0)
        s1 = jnp.where(inside, jnp.int32(b), s1)
    return s0, s1


def _pool_body(prev_ref, cur_ref, next_ref, w_ref, sc_ref, o_ref, *, ts, seqs, group):
    r0 = pl.program_id(0) * ts
    s0, s1 = _seq_bounds(r0, seqs)
    cur = cur_ref[...].astype(F32)
    prev = jnp.where(r0 > s0, prev_ref[...].astype(F32), 0.0)
    nxt = jnp.where(r0 + ts < s1, next_ref[...].astype(F32), 0.0)
    ext = jnp.concatenate([prev, cur, nxt], axis=0).astype(BF16)
    te = ts + 2 * POOL_HALO
    d = lax.broadcasted_iota(jnp.int32, (ts, te), 1) - lax.broadcasted_iota(jnp.int32, (ts, te), 0)
    rows = r0 + lax.broadcasted_iota(jnp.int32, (ts, group), 0)
    for gi, w in enumerate(POOL_WINDOWS):
        left = w // 2
        right = w - 1 - left
        band = jnp.where(d >= POOL_HALO - left, jnp.where(d <= POOL_HALO + right, 1.0, 0.0), 0.0).astype(BF16)
        sl = slice(gi * group, (gi + 1) * group)
        tot = jnp.dot(band, ext[:, sl], preferred_element_type=F32)
        lo = jnp.maximum(rows - left, s0)
        hi = jnp.minimum(rows + right, s1 - 1)
        cnt = (hi - lo + 1).astype(F32)
        p = tot / cnt - cur[:, sl]
        y = jnp.dot(p.astype(BF16), w_ref[gi], preferred_element_type=F32) * sc_ref[:, sl]
        o_ref[:, sl] = y.astype(o_ref.dtype)


def pool_mixer(zs, pool_w, pool_scale, seqs, ts=512):
    t = zs.shape[0]
    ng, group, _ = pool_w.shape
    width = ng * group
    ts = _tile(ts, min(b - a for a, b in seqs))
    hb = ts // POOL_HALO
    nrow8 = t // POOL_HALO
    body = functools.partial(_pool_body, ts=ts, seqs=tuple(seqs), group=group)
    return pl.pallas_call(
        body,
        grid=(t // ts,),
        in_specs=[pl.BlockSpec((POOL_HALO, width), lambda i: (jnp.maximum(i * hb - 1, 0), 0)),
                  pl.BlockSpec((ts, width), lambda i: (i, 0)),
                  pl.BlockSpec((POOL_HALO, width), lambda i: (jnp.minimum((i + 1) * hb, nrow8 - 1), 0)),
                  pl.BlockSpec((ng, group, group), lambda i: (0, 0, 0)),
                  pl.BlockSpec((1, width), lambda i: (0, 0))],
        out_specs=pl.BlockSpec((ts, width), lambda i: (i, 0)),
        out_shape=jax.ShapeDtypeStruct((t, width), BF16),
        compiler_params=_cparams(("parallel",)),
        name="pool_mixer",
    )(zs, zs, zs, pool_w.astype(BF16), pool_scale.reshape(1, width).astype(F32))


def _rope_combine(x, cs):
    tmp = x * cs
    return tmp + pltpu.roll(tmp, LANES // 2, axis=1)


def _qproj_body(z_ref, g_ref, w_ref, cs_ref, o_ref, *, heads, scale):
    cn = _rms(z_ref[...].astype(F32), g_ref[...]).astype(BF16)
    acc = jnp.dot(cn, w_ref[...], preferred_element_type=F32)
    cs = cs_ref[...] * scale
    for h in range(heads):
        lo = acc[:, h * HEAD_PAD:h * HEAD_PAD + LANES] * scale
        hi = _rope_combine(acc[:, h * HEAD_PAD + LANES:(h + 1) * HEAD_PAD], cs)
        o_ref[:, h * HEAD_PAD:h * HEAD_PAD + LANES] = lo.astype(o_ref.dtype)
        o_ref[:, h * HEAD_PAD + LANES:(h + 1) * HEAD_PAD] = hi.astype(o_ref.dtype)


def q_proj(zs, col_block, q_norm, wq, cs, heads, tm=512):
    t = zs.shape[0]
    r = wq.shape[0]
    tm = _tile(tm, t)
    scale = (QK_NOPE + QK_ROPE) ** -0.5 * math.log2(math.e)
    body = functools.partial(_qproj_body, heads=heads, scale=scale)
    return pl.pallas_call(
        body,
        grid=(t // tm,),
        in_specs=[pl.BlockSpec((tm, r), lambda i: (i, col_block)),
                  pl.BlockSpec((1, r), lambda i: (0, 0)),
                  pl.BlockSpec((r, heads * HEAD_PAD), lambda i: (0, 0)),
                  pl.BlockSpec((tm, LANES), lambda i: (i, 0))],
        out_specs=pl.BlockSpec((tm, heads * HEAD_PAD), lambda i: (i, 0)),
        out_shape=jax.ShapeDtypeStruct((t, heads * HEAD_PAD), BF16),
        compiler_params=_cparams(("parallel",)),
        name="q_proj",
    )(zs, q_norm.reshape(1, r).astype(F32), wq, cs)


def _kvproj_body(z_ref, g_ref, kpe_ref, wk_ref, wvt_ref, cs_ref, k_ref, vt_ref, *, heads):
    cn = _rms(z_ref[...].astype(F32), g_ref[...]).astype(BF16)
    kn = jnp.dot(cn, wk_ref[...], preferred_element_type=F32)
    vt = lax.dot_general(wvt_ref[...], cn, (((1,), (1,)), ((), ())), preferred_element_type=F32)
    kr = _rope_combine(kpe_ref[...].astype(F32), cs_ref[...])
    lane = lax.broadcasted_iota(jnp.int32, kr.shape, 1)
    kr = jnp.where(lane < QK_ROPE, kr, 0.0).astype(k_ref.dtype)
    for h in range(heads):
        k_ref[:, h * HEAD_PAD:h * HEAD_PAD + LANES] = kn[:, h * LANES:(h + 1) * LANES].astype(k_ref.dtype)
        k_ref[:, h * HEAD_PAD + LANES:(h + 1) * HEAD_PAD] = kr
    vt_ref[...] = vt.astype(vt_ref.dtype)


def kv_proj(zs, ckv_block, kpe_block, kv_norm, wk, wvt, cs, heads, tm=512):
    t = zs.shape[0]
    r = wk.shape[0]
    tm = _tile(tm, t)
    body = functools.partial(_kvproj_body, heads=heads)
    return pl.pallas_call(
        body,
        grid=(t // tm,),
        in_specs=[pl.BlockSpec((tm, r), lambda i: (i, ckv_block)),
                  pl.BlockSpec((1, r), lambda i: (0, 0)),
                  pl.BlockSpec((tm, LANES), lambda i: (i, kpe_block)),
                  pl.BlockSpec((r, heads * QK_NOPE), lambda i: (0, 0)),
                  pl.BlockSpec((heads * V_HEAD, r), lambda i: (0, 0)),
                  pl.BlockSpec((tm, LANES), lambda i: (i, 0))],
        out_specs=[pl.BlockSpec((tm, heads * HEAD_PAD), lambda i: (i, 0)),
                   pl.BlockSpec((heads * V_HEAD, tm), lambda i: (0, i))],
        out_shape=[jax.ShapeDtypeStruct((t, heads * HEAD_PAD), BF16),
                   jax.ShapeDtypeStruct((heads * V_HEAD, t), BF16)],
        compiler_params=_cparams(("parallel",)),
        name="kv_proj",
    )(zs, kv_norm.reshape(1, r).astype(F32), zs, wk, wvt, cs)


def _sublane_allreduce(x, op):
    for shift in (4, 2, 1):
        x = op(x, pltpu.roll(x, shift, axis=0))
    return x


def _attn_body(q_ref, k_ref, vt_ref, o_ref, s_scr, smax_scr, *, tk, nsplit):
    tq = q_ref.shape[0]
    tw = tq // nsplit
    nk = k_ref.shape[0] // tk
    vr = V_HEAD // SUBLANES
    qt = q_ref[...].astype(F32).T.astype(BF16)
    qts = [qt[:, i * tw:(i + 1) * tw] for i in range(nsplit)]

    def produce(j, slot):
        ks = k_ref[pl.ds(pl.multiple_of(j * tk, tk), tk), :]
        for i, qh in enumerate(qts):
            s = jnp.dot(ks, qh, preferred_element_type=F32).reshape(tk // SUBLANES, SUBLANES, tw)
            s_scr[slot, i] = s
            smax_scr[slot, i] = jnp.max(s, axis=0)

    def consume(j, slot, state):
        vts = vt_ref[:, pl.ds(pl.multiple_of(j * tk, tk), tk)]
        new = []
        for i, (m, l, acc) in enumerate(state):
            mn = jnp.maximum(m, _sublane_allreduce(smax_scr[slot, i], jnp.maximum))
            p = jnp.exp2(s_scr[slot, i] - mn[None])
            alpha = jnp.exp2(m - mn)
            l = alpha * l + jnp.sum(p, axis=0)
            pv = jnp.dot(vts, p.reshape(tk, tw).astype(BF16), preferred_element_type=F32)
            acc = alpha[None] * acc + pv.reshape(vr, SUBLANES, tw)
            new.append((mn, l, acc))
        return tuple(new)

    def pair(jj, state):
        j = 2 * jj
        produce(j + 1, 1)
        state = consume(j, 0, state)
        produce(j + 2, 0)
        return consume(j + 1, 1, state)

    state = tuple((jnp.full((SUBLANES, tw), -jnp.inf, F32), jnp.zeros((SUBLANES, tw), F32),
                   jnp.zeros((vr, SUBLANES, tw), F32)) for _ in range(nsplit))
    produce(0, 0)
    if nk == 1:
        fin = consume(0, 0, state)
    else:
        assert nk % 2 == 0
        state = lax.fori_loop(0, nk // 2 - 1, pair, state)
        produce(nk - 1, 1)
        fin = consume(nk - 1, 1, consume(nk - 2, 0, state))
    for i, (_, l, acc) in enumerate(fin):
        lt = _sublane_allreduce(l, jnp.add)
        o = (acc / lt[None]).reshape(V_HEAD, tw)
        o_ref[i * tw:(i + 1) * tw, :] = o.T.astype(o_ref.dtype)


def attention(qp, kp, vt, row0, nseq, seq, heads, tq=512, tk=512, nsplit=2):
    tq = _tile(tq, seq)
    tk = _tile(tk, seq)
    nq = seq // tq
    q0 = row0 // tq
    b0 = row0 // seq
    assert row0 % seq == 0
    return pl.pallas_call(
        functools.partial(_attn_body, tk=tk, nsplit=nsplit),
        grid=(nseq, heads, nq),
        in_specs=[pl.BlockSpec((tq, HEAD_PAD), lambda b, h, i: (q0 + b * nq + i, h)),
                  pl.BlockSpec((seq, HEAD_PAD), lambda b, h, i: (b0 + b, h)),
                  pl.BlockSpec((V_HEAD, seq), lambda b, h, i: (h, b0 + b))],
        out_specs=pl.BlockSpec((tq, V_HEAD), lambda b, h, i: (b * nq + i, h)),
        out_shape=jax.ShapeDtypeStruct((nseq * seq, heads * V_HEAD), BF16),
        scratch_shapes=[pltpu.VMEM((2, nsplit, tk // SUBLANES, SUBLANES, tq // nsplit), F32),
                        pltpu.VMEM((2, nsplit, SUBLANES, tq // nsplit), F32)],
        compiler_params=_cparams(("parallel", "parallel", "parallel")),
        name="mla_attention",
    )(qp, kp, vt)


def _s5_weights(a_re, a_im, log_dt, b_re, b_im, c_re, c_im, d_skip):
    L = SSM_CHUNK
    lam = lax.complex(a_re.astype(F32), a_im.astype(F32))
    dt = jnp.exp(log_dt.astype(F32))[..., None]
    z = lam * dt
    lam_bar = jnp.exp(z)
    ks = jnp.arange(L + 1, dtype=F32)
    pw = jnp.exp(z[..., None] * ks.astype(jnp.complex64))
    b_bar = ((lam_bar - 1.0) / lam)[..., None] * lax.complex(b_re.astype(F32), b_im.astype(F32))
    c_mat = lax.complex(c_re.astype(F32), c_im.astype(F32))
    g, n, p = b_bar.shape[1:]
    kern = jnp.einsum("dgpn,dgnk,dgnq->dgkpq", c_mat, pw[..., :L], b_bar).real

    def causal_toeplitz(k):
        kpad = jnp.concatenate([jnp.zeros((g, L - 1, p, p), F32), k], axis=1)
        return jnp.stack([kpad[:, L - 1 - lp:2 * L - 1 - lp] for lp in range(L)], axis=1)

    diag = jnp.eye(L, dtype=F32)[None, :, :, None, None]
    eye = jnp.eye(p, dtype=F32) * d_skip.astype(F32).reshape(g, 1, 1, p, 1)
    m = causal_toeplitz(kern[0]) + causal_toeplitz(kern[1]).transpose(0, 2, 1, 3, 4) + diag * eye
    m = m.transpose(0, 1, 4, 2, 3).reshape(g, L * p, L * p)

    cf = jnp.einsum("gnl,gnq->glqn", pw[0][..., :L][..., ::-1], b_bar[0])
    cb = jnp.einsum("gnl,gnq->glqn", pw[1][..., :L], b_bar[1])
    cf = cf.reshape(g, L * p, n)
    cb = cb.reshape(g, L * p, n)
    bx = jnp.concatenate([cf.real, cf.imag, cf.imag, cf.real, cb.real, cb.imag, cb.imag, cb.real], axis=-1)

    wf = jnp.einsum("gpn,gnl->gnlp", c_mat[0], pw[0][..., 1:]).reshape(g, n, L * p)
    wb = jnp.einsum("gpn,gnl->gnlp", c_mat[1], pw[1][..., 1:][..., ::-1]).reshape(g, n, L * p)
    cyf = jnp.concatenate([wf.real, -wf.imag], axis=1)
    cyb = jnp.concatenate([wb.real, -wb.imag], axis=1)

    al = pw[..., L]

    def pack(x, y):
        return jnp.concatenate([x, y], axis=-1).reshape(1, g * 2 * n)

    a1f, a2f = pack(al[0].real, al[0].real), pack(-al[0].imag, al[0].imag)
    a1b, a2b = pack(al[1].real, al[1].real), pack(-al[1].imag, al[1].imag)
    return (m.astype(BF16), bx.astype(BF16), cyf.astype(BF16), cyb.astype(BF16),
            a1f.astype(F32), a2f.astype(F32), a1b.astype(F32), a2b.astype(F32))


def _s5x_body(u_ref, bx_ref, xf_ref, xfs_ref, xb_ref, xbs_ref):
    x = jnp.dot(u_ref[0], bx_ref[0], preferred_element_type=F32)
    xf_ref[...] = x[:, 0 * LANES:1 * LANES]
    xfs_ref[...] = x[:, 1 * LANES:2 * LANES]
    xb_ref[...] = x[:, 2 * LANES:3 * LANES]
    xbs_ref[...] = x[:, 3 * LANES:4 * LANES]


def _s5scan_body(xf_ref, xfs_ref, xb_ref, xbs_ref, a1f_ref, a2f_ref, a1b_ref, a2b_ref, hf_ref, hb_ref, *,
                 seq_chunks):
    a1f, a2f, a1b, a2b = a1f_ref[...], a2f_ref[...], a1b_ref[...], a2b_ref[...]
    zero = jnp.zeros_like(a1f)
    for c0, n in seq_chunks:

        def step(i, carry, c0=c0, n=n):
            hf, hfs, hb, hbs = carry
            rf = c0 + i
            rb = c0 + n - 1 - i
            hf_ref[pl.ds(rf, 1), :] = hf
            hb_ref[pl.ds(rb, 1), :] = hb
            xf = xf_ref[pl.ds(rf, 1), :]
            xfs = xfs_ref[pl.ds(rf, 1), :]
            xb = xb_ref[pl.ds(rb, 1), :]
            xbs = xbs_ref[pl.ds(rb, 1), :]
            nhf = a1f * hf + a2f * hfs + xf
            nhfs = a1f * hfs - a2f * hf + xfs
            nhb = a1b * hb + a2b * hbs + xb
            nhbs = a1b * hbs - a2b * hb + xbs
            return nhf, nhfs, nhb, nhbs

        lax.fori_loop(0, n, step, (zero, zero, zero, zero))


def _s5y_body(u_ref, m_ref, hf_ref, hb_ref, cyf_ref, cyb_ref, y_ref):
    y = jnp.dot(u_ref[0], m_ref[0], preferred_element_type=F32)
    y = y + jnp.dot(hf_ref[...].astype(BF16), cyf_ref[0], preferred_element_type=F32)
    y = y + jnp.dot(hb_ref[...].astype(BF16), cyb_ref[0], preferred_element_type=F32)
    y_ref[0] = y


def s5_mix(u, weights, seqs):
    m, bx, cyf, cyb, a1f, a2f, a1b, a2b = weights
    t = u.shape[0]
    g = m.shape[0]
    L, p = SSM_CHUNK, SSM_GROUP
    lp = L * p
    c = t // L
    ug = u.reshape(c, L, g, p).transpose(2, 0, 1, 3).reshape(g, c, lp).astype(BF16)
    xs = pl.pallas_call(
        _s5x_body,
        grid=(g,),
        in_specs=[pl.BlockSpec((1, c, lp), lambda i: (i, 0, 0)),
                  pl.BlockSpec((1, lp, 4 * LANES), lambda i: (i, 0, 0))],
        out_specs=[pl.BlockSpec((c, LANES), lambda i: (0, i))] * 4,
        out_shape=[jax.ShapeDtypeStruct((c, g * LANES), F32)] * 4,
        compiler_params=_cparams(("parallel",)),
        name="s5_chunk_states",
    )(ug, bx)
    seq_chunks = tuple((a // L, (b - a) // L) for a, b in seqs)
    tc = _tile(512, g * LANES)
    xspec = pl.BlockSpec((c, tc), lambda i: (0, i))
    aspec = pl.BlockSpec((1, tc), lambda i: (0, i))
    hf, hb = pl.pallas_call(
        functools.partial(_s5scan_body, seq_chunks=seq_chunks),
        grid=(g * LANES // tc,),
        in_specs=[xspec] * 4 + [aspec] * 4,
        out_specs=[xspec] * 2,
        out_shape=[jax.ShapeDtypeStruct((c, g * LANES), F32)] * 2,
        compiler_params=_cparams(("parallel",)),
        name="s5_state_scan",
    )(*xs, a1f, a2f, a1b, a2b)
    yg = pl.pallas_call(
        _s5y_body,
        grid=(g,),
        in_specs=[pl.BlockSpec((1, c, lp), lambda i: (i, 0, 0)),
                  pl.BlockSpec((1, lp, lp), lambda i: (i, 0, 0)),
                  pl.BlockSpec((c, LANES), lambda i: (0, i)),
                  pl.BlockSpec((c, LANES), lambda i: (0, i)),
                  pl.BlockSpec((1, LANES, lp), lambda i: (i, 0, 0)),
                  pl.BlockSpec((1, LANES, lp), lambda i: (i, 0, 0))],
        out_specs=pl.BlockSpec((1, c, lp), lambda i: (i, 0, 0)),
        out_shape=jax.ShapeDtypeStruct((g, c, lp), F32),
        compiler_params=_cparams(("parallel",)),
        name="s5_outputs",
    )(ug, m, hf, hb, cyf, cyb)
    return yg.reshape(g, c, L, p).transpose(1, 2, 0, 3).reshape(t, g * p)


def _glu_body(y_ref, w_ref, o_ref):
    gl = _gelu_tanh(y_ref[...])
    zz = jnp.dot(gl.astype(BF16), w_ref[...], preferred_element_type=F32)
    o_ref[...] = (gl * jax.nn.sigmoid(zz)).astype(o_ref.dtype)


def glu(y, w_glu, tm=512):
    t, w = y.shape
    tm = _tile(tm, t)
    return pl.pallas_call(
        _glu_body,
        grid=(t // tm,),
        in_specs=[pl.BlockSpec((tm, w), lambda i: (i, 0)),
                  pl.BlockSpec((w, w), lambda i: (0, 0))],
        out_specs=pl.BlockSpec((tm, w), lambda i: (i, 0)),
        out_shape=jax.ShapeDtypeStruct((t, w), BF16),
        compiler_params=_cparams(("parallel",)),
        name="s5_glu",
    )(y, w_glu)


def _merge_body(ya_ref, yb_ref, yc_ref, pa_ref, pb_ref, pc_ref, ga_ref, gb_ref, gc_ref, o_ref):
    a = jnp.dot(ya_ref[...], pa_ref[...], preferred_element_type=F32)
    b = jnp.dot(yb_ref[...], pb_ref[...], preferred_element_type=F32)
    c = jnp.dot(yc_ref[...], pc_ref[...], preferred_element_type=F32)
    o = ga_ref[...].astype(F32) * a + gb_ref[...].astype(F32) * b + gc_ref[...].astype(F32) * c
    o_ref[...] = o.astype(o_ref.dtype)


def gated_merge(ya, yb, yc, pa, pb, pc, gates, tm=512, tn=1024):
    t = ya.shape[0]
    d = pa.shape[1]
    tm = _tile(tm, t)
    tn = _tile(tn, d)
    nj = d // tn

    def aspec(x):
        return pl.BlockSpec((tm, x.shape[1]), lambda i, j: (i, 0))

    def wspec(x):
        return pl.BlockSpec((x.shape[0], tn), lambda i, j: (0, j))

    gspecs = [pl.BlockSpec((tm, tn), functools.partial(lambda i, j, k: (i, k * nj + j), k=k)) for k in range(3)]
    return pl.pallas_call(
        _merge_body,
        grid=(t // tm, nj),
        in_specs=[aspec(ya), aspec(yb), aspec(yc), wspec(pa), wspec(pb), wspec(pc)] + gspecs,
        out_specs=pl.BlockSpec((tm, tn), lambda i, j: (i, j)),
        out_shape=jax.ShapeDtypeStruct((t, d), BF16),
        compiler_params=_cparams(("parallel", "parallel")),
        name="gated_merge",
    )(ya, yb, yc, pa, pb, pc, gates, gates, gates)


def _oddeven_merge(lo, hi, r):
    step = r * 2
    if step < hi - lo:
        yield from _oddeven_merge(lo, hi, step)
        yield from _oddeven_merge(lo + r, hi, step)
        yield from [(i, i + r) for i in range(lo + r, hi - r, step)]
    else:
        yield (lo, lo + r)


def _oddeven_merge_sort(lo, hi):
    if hi - lo >= 1:
        mid = lo + (hi - lo) // 2
        yield from _oddeven_merge_sort(lo, mid)
        yield from _oddeven_merge_sort(mid + 1, hi)
        yield from _oddeven_merge(lo, hi, 1)


_SORT16 = tuple(_oddeven_merge_sort(0, PEER_TOPK - 1))
_BITONIC16 = tuple((i, i + d) for d in (8, 4, 2, 1) for i in range(PEER_TOPK) if not i & d)


def _compare_exchange(v, pairs):
    for i, j in pairs:
        hi = jnp.maximum(v[i], v[j])
        lo = jnp.minimum(v[i], v[j])
        v[i], v[j] = hi, lo


def _top16_desc(v):
    v = list(v)
    _compare_exchange(v, _SORT16)
    for shift in (4, 2, 1):
        r = [pltpu.roll(x, shift, axis=0) for x in v]
        v = [jnp.maximum(v[i], r[PEER_TOPK - 1 - i]) for i in range(PEER_TOPK)]
        _compare_exchange(v, _BITONIC16)
    return v


def _peer_select_body(q_ref, keys_ref, s1_ref, s2_ref, e2_ref, tau_ref, m1_ref, *, heads):
    tt = q_ref.shape[0]
    sub = lax.broadcasted_iota(jnp.int32, (SUBLANES, tt), 0)
    ninf = jnp.full((SUBLANES, tt), -jnp.inf, F32)

    def by_sublane(lst):
        out = lst[SUBLANES - 1]
        for j in range(SUBLANES - 2, -1, -1):
            out = jnp.where(sub == j, lst[j], out)
        return out

    for h in range(heads):
        scores, tops = [], []
        for c in range(2):
            blk = 2 * h + c
            qb = q_ref[:, blk * N_KEYS:(blk + 1) * N_KEYS]
            kb = keys_ref[blk]
            q_hi, k_hi = qb.astype(BF16), kb.astype(BF16)
            q_lo = (qb - q_hi.astype(F32)).astype(BF16)
            k_lo = (kb - k_hi.astype(F32)).astype(BF16)
            nt = (((1,), (1,)), ((), ()))
            st = (lax.dot_general(k_hi, q_hi, nt, preferred_element_type=F32)
                  + lax.dot_general(k_hi, q_lo, nt, preferred_element_type=F32)
                  + lax.dot_general(k_lo, q_hi, nt, preferred_element_type=F32))
            scores.append(st)
            tops.append(_top16_desc([st[SUBLANES * a:SUBLANES * (a + 1), :] for a in range(N_KEYS // SUBLANES)]))
        t1, t2 = tops
        t2a, t2b, t1b = by_sublane(t2[:8]), by_sublane(t2[8:]), by_sublane(t1[8:])
        cand = [t1[0] + t2a, t1[0] + t2b, t1[1] + t2a,
                jnp.where(sub < 5, t1[2] + t2a, ninf), jnp.where(sub < 4, t1[3] + t2a, ninf),
                jnp.where(sub < 3, t1[4] + t2a, ninf), jnp.where(sub < 2, t1[5] + t2a, ninf),
                jnp.where(sub < 2, t1[6] + t2a, ninf), jnp.where(sub < 2, t1[7] + t2a, ninf),
                t1b + t2[0]] + [ninf] * 6
        best = _top16_desc(cand)
        z = jnp.exp(best[0] - best[0])
        for b in best[1:]:
            z = z + jnp.exp(b - best[0])
        s1_ref[h] = scores[0]
        s2_ref[h] = scores[1]
        e2_ref[h] = jnp.exp(scores[1] - t2[0][0:1, :]) / z[0:1, :]
        tau_ref[h:h + 1, :] = best[PEER_TOPK - 1][0:1, :]
        m1_ref[h:h + 1, :] = t1[0][0:1, :]


def peer_select(q, keys, heads, tt=256):
    t = q.shape[0]
    tt = _tile(tt, t)
    big = jax.ShapeDtypeStruct((heads, N_KEYS, t), F32)
    small = jax.ShapeDtypeStruct((heads, t), F32)
    bspec = pl.BlockSpec((heads, N_KEYS, tt), lambda i: (0, 0, i))
    sspec = pl.BlockSpec((heads, tt), lambda i: (0, i))
    return pl.pallas_call(
        functools.partial(_peer_select_body, heads=heads),
        grid=(t // tt,),
        in_specs=[pl.BlockSpec((tt, q.shape[1]), lambda i: (i, 0)),
                  pl.BlockSpec(keys.shape, lambda i: (0, 0, 0))],
        out_specs=[bspec, bspec, bspec, sspec, sspec],
        out_shape=[big, big, big, small, small],
        compiler_params=_cparams(("parallel",)),
        name="peer_select",
    )(q, keys)


def _peer_dense_body(x_ref, h_ref, u_ref, v_ref, sc_ref, s1_ref, s2_ref, e2_ref, tau_ref, m1_ref, o_ref, *, heads):
    j = pl.program_id(1)
    te = u_ref.shape[0]
    na = te // N_KEYS

    @pl.when(j == 0)
    def _():
        o_ref[...] = x_ref[...]

    act = lax.dot_general(u_ref[...], h_ref[...], (((1,), (1,)), ((), ())), preferred_element_type=F32)
    act = act * sc_ref[:, 0:1]
    rows = []
    for al in range(na):
        a = j * na + al
        g = None
        for h in range(heads):
            s1row = s1_ref[h, pl.ds(a, 1), :]
            e1row = jnp.exp(s1row - m1_ref[h:h + 1, :])
            sel = (s1row + s2_ref[h]) >= tau_ref[h:h + 1, :]
            contrib = jnp.where(sel, e1row * e2_ref[h], 0.0)
            g = contrib if g is None else g + contrib
        rows.append(g)
    gate = rows[0] if na == 1 else jnp.concatenate(rows, axis=0)
    wgt = gate * _gelu_tanh(act)
    peak = jnp.max(jnp.abs(wgt), axis=(0, 1), keepdims=True)
    ws = jnp.where(peak > 0.0, jnp.exp2(jnp.floor(jnp.log2(F8_TARGET / jnp.where(peak > 0.0, peak, 1.0)))), 1.0)
    part = lax.dot_general((wgt * ws).astype(F8), v_ref[...], (((0,), (0,)), ((), ())),
                           preferred_element_type=F32)
    o_ref[...] += part * (sc_ref[:, 1:2] / ws)


def peer_dense(x, h, u_tab, v_tab, layer, scales, s1, s2, e2, tau, m1, heads, tm=512, te=1024):
    t, d = h.shape
    e = u_tab.shape[1]
    tm = _tile(tm, t)
    te = _tile(te, e)
    once = pl.Buffered(1)
    bspec = pl.BlockSpec((heads, N_KEYS, tm), lambda i, j: (0, 0, i), pipeline_mode=once)
    sspec = pl.BlockSpec((heads, tm), lambda i, j: (0, i))
    return pl.pallas_call(
        functools.partial(_peer_dense_body, heads=heads),
        grid=(t // tm, e // te),
        in_specs=[pl.BlockSpec((tm, d), lambda i, j: (i, 0), pipeline_mode=once),
                  pl.BlockSpec((tm, d), lambda i, j: (i, 0), pipeline_mode=once),
                  pl.BlockSpec((None, te, d), lambda i, j: (layer, j, 0)),
                  pl.BlockSpec((None, te, d), lambda i, j: (layer, j, 0)),
                  pl.BlockSpec((1, 2), lambda i, j: (0, 0)),
                  bspec, bspec, bspec, sspec, sspec],
        out_specs=pl.BlockSpec((tm, d), lambda i, j: (i, 0)),
        out_shape=jax.ShapeDtypeStruct((t, d), F32),
        compiler_params=_cparams(("parallel", "arbitrary")),
        name="peer_dense",
    )(x, h, u_tab, v_tab, scales.astype(F32), s1, s2, e2, tau, m1)


def _rope_table(seqs):
    inv = 1.0 / (ROPE_BASE ** (jnp.arange(0, QK_ROPE, 2, dtype=F32) / QK_ROPE))
    pos = jnp.concatenate([jnp.arange(b - a, dtype=F32) for a, b in seqs])
    ang = pos[:, None] * inv[None, :]
    c, s = jnp.cos(ang), jnp.sin(ang)
    return jnp.concatenate([c, c, s, s], axis=-1)


def _rotate_half_cols(w):
    half = w.shape[-1] // 2
    return jnp.concatenate([-w[..., half:], w[..., :half]], axis=-1)


def _mixer(x, seqs, cs, norm_mix, w_in, pool_w, pool_scale, q_norm, w_uq, kv_norm, w_ukv,
           a_re, a_im, log_dt, b_re, b_im, c_re, c_im, d_skip, w_glu, p_a, p_b, p_c, w_o):
    t, d = x.shape
    pool_width = pool_w.shape[0] * pool_w.shape[1]
    q_lora = w_uq.shape[0]
    kv_lora = w_ukv.shape[0]
    heads = w_uq.shape[1]
    ssm_width = d_skip.shape[0]
    o1 = pool_width
    o2 = o1 + q_lora
    o3 = o2 + kv_lora
    o4 = o3 + QK_ROPE
    o5 = o4 + ssm_width

    w_kpe = w_in[:, o3:o4]
    small_cols = [w_in[:, :o1], w_in[:, o1:o2], w_in[:, o4:o5], w_in[:, o2:o3], w_kpe, _rotate_half_cols(w_kpe)]
    n_small = o5 + QK_ROPE
    tn_small = 1280
    n_pad = -n_small % tn_small
    w_small = jnp.concatenate(small_cols + [jnp.zeros((d, n_pad), w_in.dtype)], axis=1).astype(BF16)
    hn, hn8, s_h = rmsnorm(x, norm_mix, BF16, with_f8=True)
    zs = matmul(hn, w_small, F32, tm=512, tn=tn_small, name="in_proj_small")
    wg8, s_w = _to_f8(w_in[:, o5:])
    gates = matmul(hn8, wg8, BF16, tm=1024, tn=1024, act="sigmoid", out_scale=1.0 / (s_h * s_w),
                   name="in_proj_gates")
    cq_block = o1 // q_lora
    ssm_block = o2 // ssm_width
    ckv_block = (o2 + ssm_width) // kv_lora
    kpe_block = (o2 + ssm_width + kv_lora) // LANES
    assert o1 % q_lora == 0 and o2 % ssm_width == 0 and (o2 + ssm_width) % kv_lora == 0

    ya = pool_mixer(zs, pool_w, pool_scale, seqs)

    wq_nope = w_uq[:, :, :QK_NOPE]
    wq_pe = w_uq[:, :, QK_NOPE:]
    wq = jnp.concatenate([wq_nope, wq_pe, _rotate_half_cols(wq_pe)], axis=-1).reshape(q_lora, heads * HEAD_PAD)
    qp = q_proj(zs, cq_block, q_norm, wq.astype(BF16), cs, heads)
    wk = w_ukv[:, :, :QK_NOPE].reshape(kv_lora, heads * QK_NOPE).astype(BF16)
    wvt = w_ukv[:, :, QK_NOPE:].reshape(kv_lora, heads * V_HEAD).T.astype(BF16)
    kp, v = kv_proj(zs, ckv_block, kpe_block, kv_norm, wk, wvt, cs, heads)
    groups = []
    for a, b in seqs:
        if groups and groups[-1][2] == b - a and groups[-1][0] + groups[-1][1] * groups[-1][2] == a:
            groups[-1][1] += 1
        else:
            groups.append([a, 1, b - a])
    outs = [attention(qp, kp, v, row0, nseq, seq, heads) for row0, nseq, seq in groups]
    yb = outs[0] if len(outs) == 1 else jnp.concatenate(outs, axis=0)

    u_ssm = lax.slice_in_dim(zs, ssm_block * ssm_width, (ssm_block + 1) * ssm_width, axis=1)
    s5w = _s5_weights(a_re, a_im, log_dt, b_re, b_im, c_re, c_im, d_skip)
    yc = glu(s5_mix(u_ssm, s5w, seqs), w_glu.astype(BF16))

    merged = gated_merge(ya, yb, yc, p_a.astype(BF16), p_b.astype(BF16), p_c.astype(BF16), gates)
    return matmul(merged, w_o.astype(BF16), F32, tm=1024, tn=512, residual=x, name="out_proj")


def _peer(x, layer, norm_ffn, peer_wq, peer_keys, u_tabs, u_scales, v_tabs, v_scales):
    h2, h8, s_h = rmsnorm(x, norm_ffn, BF16, with_f8=True)
    pheads = peer_keys.shape[0]
    q = matmul(h2, peer_wq.astype(BF16), F32, tm=1024, tn=1024, name="peer_query")
    keys = peer_keys.reshape(pheads * 2, N_KEYS, peer_keys.shape[-1]).astype(F32)
    s1, s2, e2, tau, m1 = peer_select(q, keys, pheads)
    scales = jnp.stack([1.0 / (s_h * u_scales[layer]), 1.0 / v_scales[layer]]).reshape(1, 2)
    return peer_dense(x, h8, u_tabs, v_tabs, layer, scales, s1, s2, e2, tau, m1, pheads)


def kernel(x_prompt, x_sample, norm_mix, w_in, pool_w, pool_scale, q_norm, w_uq, kv_norm, w_ukv, ssm_a_re, ssm_a_im, ssm_log_dt, ssm_b_re, ssm_b_im, ssm_c_re, ssm_c_im, ssm_d, w_glu, p_a, p_b, p_c, w_o, norm_ffn, peer_wq, peer_keys, peer_u, peer_v, final_norm):
    d = x_prompt.shape[-1]
    seqs = []
    row = 0
    for xs in (x_prompt, x_sample):
        for _ in range(xs.shape[0]):
            seqs.append((row, row + xs.shape[1]))
            row += xs.shape[1]
    x = jnp.concatenate([x_prompt.reshape(-1, d), x_sample.reshape(-1, d)], axis=0)
    cs = _rope_table(seqs)
    per_layer = (norm_mix, w_in.astype(BF16), pool_w, pool_scale, q_norm, w_uq, kv_norm, w_ukv, ssm_a_re, ssm_a_im, ssm_log_dt,
                 ssm_b_re, ssm_b_im, ssm_c_re, ssm_c_im, ssm_d, w_glu, p_a, p_b, p_c, w_o, norm_ffn, peer_wq,
                 peer_keys)
    n_mix = 21
    u_scales = _pow2_scale(jnp.max(jnp.abs(peer_u), axis=(1, 2)))
    v_scales = _pow2_scale(jnp.max(jnp.abs(peer_v), axis=(1, 2)))
    u_tabs = (peer_u * u_scales[:, None, None]).astype(F8)
    v_tabs = (peer_v * v_scales[:, None, None]).astype(F8)
    for layer in range(norm_mix.shape[0]):
        lw = [w[layer] for w in per_layer]
        x = _mixer(x, seqs, cs, *lw[:n_mix])
        x = _peer(x, layer, *lw[n_mix:], u_tabs, u_scales, v_tabs, v_scales)
    n_prompt = x_prompt.shape[0] * x_prompt.shape[1]
    y_prompt = rmsnorm(x, final_norm, x_prompt.dtype, row0=0, nrows=n_prompt)
    y_sample = rmsnorm(x, final_norm, x_sample.dtype, row0=n_prompt, nrows=x.shape[0] - n_prompt)
    return (y_prompt.reshape(x_prompt.shape), y_sample.reshape(x_sample.shape))
```

```python
import functools
import math

import jax
import jax.numpy as jnp
from jax import lax
from jax.experimental import pallas as pl
from jax.experimental.pallas import tpu as pltpu

F32 = jnp.float32
BF16 = jnp.bfloat16
F8 = jnp.float8_e4m3fn
F8_TARGET = 240.0

EPS = 1e-6
POOL_WINDOWS = (2, 4, 8, 16)
QK_NOPE = 128
QK_ROPE = 64
V_HEAD = 128
ROPE_BASE = 10000.0
SSM_GROUP = 16
SSM_STATE = 64
PEER_TOPK = 16
N_KEYS = 128

LANES = 128
SUBLANES = 8
HEAD_PAD = 256
SSM_CHUNK = 16
VMEM_LIMIT = 56 * 1024 * 1024
POOL_HALO = 8


def _cparams(sem, flags=None):
    return pltpu.CompilerParams(dimension_semantics=sem, vmem_limit_bytes=VMEM_LIMIT, flags=flags)


def _tile(pref, dim):
    t = min(pref, dim)
    assert dim % t == 0, (pref, dim)
    return t


def _gelu_tanh(x):
    cdf = 0.5 * (1.0 + jnp.tanh(math.sqrt(2.0 / math.pi) * (x + 0.044715 * (x * x * x))))
    return x * cdf


def _rms(x, g):
    y = x * lax.rsqrt(jnp.mean(x * x, axis=-1, keepdims=True) + EPS)
    return y * g


def _pow2_scale(bound):
    safe = jnp.where(bound > 0, bound, 1.0).astype(F32)
    return jnp.where(bound > 0, jnp.exp2(jnp.floor(jnp.log2(F8_TARGET / safe))), 1.0).astype(F32)


def _to_f8(w):
    s = _pow2_scale(jnp.max(jnp.abs(w)))
    return (w.astype(F32) * s).astype(F8), s


def _rmsnorm_body(x_ref, g_ref, o_ref):
    o_ref[...] = _rms(x_ref[...].astype(F32), g_ref[...]).astype(o_ref.dtype)


def _rmsnorm_f8_body(x_ref, g_ref, s_ref, o_ref, o8_ref):
    y = _rms(x_ref[...].astype(F32), g_ref[...])
    o_ref[...] = y.astype(o_ref.dtype)
    o8_ref[...] = (y * s_ref[...]).astype(o8_ref.dtype)


def rmsnorm(x, g, out_dtype, tm=512, row0=0, nrows=None, with_f8=False):
    d = x.shape[1]
    t = x.shape[0] if nrows is None else nrows
    tm = _tile(tm, math.gcd(t, row0) if row0 else t)
    b0 = row0 // tm
    xspec = pl.BlockSpec((tm, d), lambda i: (b0 + i, 0))
    gspec = pl.BlockSpec((1, d), lambda i: (0, 0))
    ospec = pl.BlockSpec((tm, d), lambda i: (i, 0))
    g2 = g.reshape(1, d).astype(F32)
    if not with_f8:
        return pl.pallas_call(
            _rmsnorm_body,
            grid=(t // tm,),
            in_specs=[xspec, gspec],
            out_specs=ospec,
            out_shape=jax.ShapeDtypeStruct((t, d), out_dtype),
            compiler_params=_cparams(("parallel",)),
            name="rmsnorm",
        )(x, g2)
    s = _pow2_scale(math.sqrt(d) * jnp.max(jnp.abs(g2)))
    y, y8 = pl.pallas_call(
        _rmsnorm_f8_body,
        grid=(t // tm,),
        in_specs=[xspec, gspec, pl.BlockSpec((1, 1), lambda i: (0, 0))],
        out_specs=[ospec, ospec],
        out_shape=[jax.ShapeDtypeStruct((t, d), out_dtype), jax.ShapeDtypeStruct((t, d), F8)],
        compiler_params=_cparams(("parallel",)),
        name="rmsnorm_f8",
    )(x, g2, s.reshape(1, 1))
    return y, y8, s


def _mm_body(a_ref, b_ref, o_ref, *, act):
    acc = jnp.dot(a_ref[...], b_ref[...], preferred_element_type=F32)
    if act == "sigmoid":
        acc = jax.nn.sigmoid(acc)
    o_ref[...] = acc.astype(o_ref.dtype)


def _mm_scaled_body(a_ref, b_ref, s_ref, o_ref, *, act):
    acc = jnp.dot(a_ref[...], b_ref[...], preferred_element_type=F32) * s_ref[...]
    if act == "sigmoid":
        acc = jax.nn.sigmoid(acc)
    o_ref[...] = acc.astype(o_ref.dtype)


def _mm_res_body(a_ref, b_ref, r_ref, o_ref):
    acc = jnp.dot(a_ref[...], b_ref[...], preferred_element_type=F32)
    o_ref[...] = (r_ref[...] + acc).astype(o_ref.dtype)


def matmul(a, b, out_dtype, tm, tn, act=None, residual=None, out_scale=None, name="matmul"):
    m, k = a.shape
    n = b.shape[1]
    tm = _tile(tm, m)
    tn = _tile(tn, n)
    in_specs = [pl.BlockSpec((tm, k), lambda i, j: (i, 0)),
                pl.BlockSpec((k, tn), lambda i, j: (0, j))]
    args = [a, b]
    if out_scale is not None:
        body = functools.partial(_mm_scaled_body, act=act)
        in_specs.append(pl.BlockSpec((1, 1), lambda i, j: (0, 0)))
        args.append(out_scale.reshape(1, 1).astype(F32))
    elif residual is None:
        body = functools.partial(_mm_body, act=act)
    else:
        body = _mm_res_body
        in_specs.append(pl.BlockSpec((tm, tn), lambda i, j: (i, j)))
        args.append(residual)
    return pl.pallas_call(
        body,
        grid=(m // tm, n // tn),
        in_specs=in_specs,
        out_specs=pl.BlockSpec((tm, tn), lambda i, j: (i, j)),
        out_shape=jax.ShapeDtypeStruct((m, n), out_dtype),
        compiler_params=_cparams(("parallel", "parallel")),
        name=name,
    )(*args)


def _seq_bounds(r0, seqs):
    s0 = jnp.int32(0)
    s1 = jnp.int32(0)
    for a, b in seqs:
        inside = jnp.logical_and(r0 >= a, r0 < b)
        s0 = jnp.where(inside, jnp.int32(a), s0)
        s1 = jnp.where(inside, jnp.int32(b), s1)
    return s0, s1


def _pool_body(prev_ref, cur_ref, next_ref, w_ref, sc_ref, o_ref, *, ts, seqs, group):
    r0 = pl.program_id(0) * ts
    s0, s1 = _seq_bounds(r0, seqs)
    cur = cur_ref[...].astype(F32)
    prev = jnp.where(r0 > s0, prev_ref[...].astype(F32), 0.0)
    nxt = jnp.where(r0 + ts < s1, next_ref[...].astype(F32), 0.0)
    ext = jnp.concatenate([prev, cur, nxt], axis=0).astype(BF16)
    te = ts + 2 * POOL_HALO
    d = lax.broadcasted_iota(jnp.int32, (ts, te), 1) - lax.broadcasted_iota(jnp.int32, (ts, te), 0)
    rows = r0 + lax.broadcasted_iota(jnp.int32, (ts, group), 0)
    for gi, w in enumerate(POOL_WINDOWS):
        left = w // 2
        right = w - 1 - left
        band = jnp.where(d >= POOL_HALO - left, jnp.where(d <= POOL_HALO + right, 1.0, 0.0), 0.0).astype(BF16)
        sl = slice(gi * group, (gi + 1) * group)
        tot = jnp.dot(band, ext[:, sl], preferred_element_type=F32)
        lo = jnp.maximum(rows - left, s0)
        hi = jnp.minimum(rows + right, s1 - 1)
        cnt = (hi - lo + 1).astype(F32)
        p = tot / cnt - cur[:, sl]
        y = jnp.dot(p.astype(BF16), w_ref[gi], preferred_element_type=F32) * sc_ref[:, sl]
        o_ref[:, sl] = y.astype(o_ref.dtype)


def pool_mixer(zs, pool_w, pool_scale, seqs, ts=512):
    t = zs.shape[0]
    ng, group, _ = pool_w.shape
    width = ng * group
    ts = _tile(ts, min(b - a for a, b in seqs))
    hb = ts // POOL_HALO
    nrow8 = t // POOL_HALO
    body = functools.partial(_pool_body, ts=ts, seqs=tuple(seqs), group=group)
    return pl.pallas_call(
        body,
        grid=(t // ts,),
        in_specs=[pl.BlockSpec((POOL_HALO, width), lambda i: (jnp.maximum(i * hb - 1, 0), 0)),
                  pl.BlockSpec((ts, width), lambda i: (i, 0)),
                  pl.BlockSpec((POOL_HALO, width), lambda i: (jnp.minimum((i + 1) * hb, nrow8 - 1), 0)),
                  pl.BlockSpec((ng, group, group), lambda i: (0, 0, 0)),
                  pl.BlockSpec((1, width), lambda i: (0, 0))],
        out_specs=pl.BlockSpec((ts, width), lambda i: (i, 0)),
        out_shape=jax.ShapeDtypeStruct((t, width), BF16),
        compiler_params=_cparams(("parallel",)),
        name="pool_mixer",
    )(zs, zs, zs, pool_w.astype(BF16), pool_scale.reshape(1, width).astype(F32))


def _rope_combine(x, cs):
    tmp = x * cs
    return tmp + pltpu.roll(tmp, LANES // 2, axis=1)


def _qproj_body(z_ref, g_ref, w_ref, cs_ref, o_ref, *, heads, scale):
    cn = _rms(z_ref[...].astype(F32), g_ref[...]).astype(BF16)
    acc = jnp.dot(cn, w_ref[...], preferred_element_type=F32)
    cs = cs_ref[...] * scale
    for h in range(heads):
        lo = acc[:, h * HEAD_PAD:h * HEAD_PAD + LANES] * scale
        hi = _rope_combine(acc[:, h * HEAD_PAD + LANES:(h + 1) * HEAD_PAD], cs)
        o_ref[:, h * HEAD_PAD:h * HEAD_PAD + LANES] = lo.astype(o_ref.dtype)
        o_ref[:, h * HEAD_PAD + LANES:(h + 1) * HEAD_PAD] = hi.astype(o_ref.dtype)


def q_proj(zs, col_block, q_norm, wq, cs, heads, tm=512):
    t = zs.shape[0]
    r = wq.shape[0]
    tm = _tile(tm, t)
    scale = (QK_NOPE + QK_ROPE) ** -0.5 * math.log2(math.e)
    body = functools.partial(_qproj_body, heads=heads, scale=scale)
    return pl.pallas_call(
        body,
        grid=(t // tm,),
        in_specs=[pl.BlockSpec((tm, r), lambda i: (i, col_block)),
                  pl.BlockSpec((1, r), lambda i: (0, 0)),
                  pl.BlockSpec((r, heads * HEAD_PAD), lambda i: (0, 0)),
                  pl.BlockSpec((tm, LANES), lambda i: (i, 0))],
        out_specs=pl.BlockSpec((tm, heads * HEAD_PAD), lambda i: (i, 0)),
        out_shape=jax.ShapeDtypeStruct((t, heads * HEAD_PAD), BF16),
        compiler_params=_cparams(("parallel",)),
        name="q_proj",
    )(zs, q_norm.reshape(1, r).astype(F32), wq, cs)


def _kvproj_body(z_ref, g_ref, kpe_ref, wk_ref, wvt_ref, cs_ref, k_ref, vt_ref, *, heads):
    cn = _rms(z_ref[...].astype(F32), g_ref[...]).astype(BF16)
    kn = jnp.dot(cn, wk_ref[...], preferred_element_type=F32)
    vt = lax.dot_general(wvt_ref[...], cn, (((1,), (1,)), ((), ())), preferred_element_type=F32)
    kr = _rope_combine(kpe_ref[...].astype(F32), cs_ref[...])
    lane = lax.broadcasted_iota(jnp.int32, kr.shape, 1)
    kr = jnp.where(lane < QK_ROPE, kr, 0.0).astype(k_ref.dtype)
    for h in range(heads):
        k_ref[:, h * HEAD_PAD:h * HEAD_PAD + LANES] = kn[:, h * LANES:(h + 1) * LANES].astype(k_ref.dtype)
        k_ref[:, h * HEAD_PAD + LANES:(h + 1) * HEAD_PAD] = kr
    vt_ref[...] = vt.astype(vt_ref.dtype)


def kv_proj(zs, ckv_block, kpe_block, kv_norm, wk, wvt, cs, heads, tm=512):
    t = zs.shape[0]
    r = wk.shape[0]
    tm = _tile(tm, t)
    body = functools.partial(_kvproj_body, heads=heads)
    return pl.pallas_call(
        body,
        grid=(t // tm,),
        in_specs=[pl.BlockSpec((tm, r), lambda i: (i, ckv_block)),
                  pl.BlockSpec((1, r), lambda i: (0, 0)),
                  pl.BlockSpec((tm, LANES), lambda i: (i, kpe_block)),
                  pl.BlockSpec((r, heads * QK_NOPE), lambda i: (0, 0)),
                  pl.BlockSpec((heads * V_HEAD, r), lambda i: (0, 0)),
                  pl.BlockSpec((tm, LANES), lambda i: (i, 0))],
        out_specs=[pl.BlockSpec((tm, heads * HEAD_PAD), lambda i: (i, 0)),
                   pl.BlockSpec((heads * V_HEAD, tm), lambda i: (0, i))],
        out_shape=[jax.ShapeDtypeStruct((t, heads * HEAD_PAD), BF16),
                   jax.ShapeDtypeStruct((heads * V_HEAD, t), BF16)],
        compiler_params=_cparams(("parallel",)),
        name="kv_proj",
    )(zs, kv_norm.reshape(1, r).astype(F32), zs, wk, wvt, cs)


def _sublane_allreduce(x, op):
    for shift in (4, 2, 1):
        x = op(x, pltpu.roll(x, shift, axis=0))
    return x


def _attn_body(q_ref, k_ref, vt_ref, o_ref, s_scr, smax_scr, *, tk, nsplit):
    tq = q_ref.shape[0]
    tw = tq // nsplit
    nk = k_ref.shape[0] // tk
    vr = V_HEAD // SUBLANES
    qt = q_ref[...].astype(F32).T.astype(BF16)
    qts = [qt[:, i * tw:(i + 1) * tw] for i in range(nsplit)]

    def produce(j, slot):
        ks = k_ref[pl.ds(pl.multiple_of(j * tk, tk), tk), :]
        for i, qh in enumerate(qts):
            s = jnp.dot(ks, qh, preferred_element_type=F32).reshape(tk // SUBLANES, SUBLANES, tw)
            s_scr[slot, i] = s
            smax_scr[slot, i] = jnp.max(s, axis=0)

    def consume(j, slot, state):
        vts = vt_ref[:, pl.ds(pl.multiple_of(j * tk, tk), tk)]
        new = []
        for i, (m, l, acc) in enumerate(state):
            mn = jnp.maximum(m, _sublane_allreduce(smax_scr[slot, i], jnp.maximum))
            p = jnp.exp2(s_scr[slot, i] - mn[None])
            alpha = jnp.exp2(m - mn)
            l = alpha * l + jnp.sum(p, axis=0)
            pv = jnp.dot(vts, p.reshape(tk, tw).astype(BF16), preferred_element_type=F32)
            acc = alpha[None] * acc + pv.reshape(vr, SUBLANES, tw)
            new.append((mn, l, acc))
        return tuple(new)

    def pair(jj, state):
        j = 2 * jj
        produce(j + 1, 1)
        state = consume(j, 0, state)
        produce(j + 2, 0)
        return consume(j + 1, 1, state)

    state = tuple((jnp.full((SUBLANES, tw), -jnp.inf, F32), jnp.zeros((SUBLANES, tw), F32),
                   jnp.zeros((vr, SUBLANES, tw), F32)) for _ in range(nsplit))
    produce(0, 0)
    if nk == 1:
        fin = consume(0, 0, state)
    else:
        assert nk % 2 == 0
        state = lax.fori_loop(0, nk // 2 - 1, pair, state)
        produce(nk - 1, 1)
        fin = consume(nk - 1, 1, consume(nk - 2, 0, state))
    for i, (_, l, acc) in enumerate(fin):
        lt = _sublane_allreduce(l, jnp.add)
        o = (acc / lt[None]).reshape(V_HEAD, tw)
        o_ref[i * tw:(i + 1) * tw, :] = o.T.astype(o_ref.dtype)


def attention(qp, kp, vt, row0, nseq, seq, heads, tq=512, tk=512, nsplit=2):
    tq = _tile(tq, seq)
    tk = _tile(tk, seq)
    nq = seq // tq
    q0 = row0 // tq
    b0 = row0 // seq
    assert row0 % seq == 0
    return pl.pallas_call(
        functools.partial(_attn_body, tk=tk, nsplit=nsplit),
        grid=(nseq, heads, nq),
        in_specs=[pl.BlockSpec((tq, HEAD_PAD), lambda b, h, i: (q0 + b * nq + i, h)),
                  pl.BlockSpec((seq, HEAD_PAD), lambda b, h, i: (b0 + b, h)),
                  pl.BlockSpec((V_HEAD, seq), lambda b, h, i: (h, b0 + b))],
        out_specs=pl.BlockSpec((tq, V_HEAD), lambda b, h, i: (b * nq + i, h)),
        out_shape=jax.ShapeDtypeStruct((nseq * seq, heads * V_HEAD), BF16),
        scratch_shapes=[pltpu.VMEM((2, nsplit, tk // SUBLANES, SUBLANES, tq // nsplit), F32),
                        pltpu.VMEM((2, nsplit, SUBLANES, tq // nsplit), F32)],
        compiler_params=_cparams(("parallel", "parallel", "parallel")),
        name="mla_attention",
    )(qp, kp, vt)


def _s5_weights(a_re, a_im, log_dt, b_re, b_im, c_re, c_im, d_skip):
    L = SSM_CHUNK
    lam = lax.complex(a_re.astype(F32), a_im.astype(F32))
    dt = jnp.exp(log_dt.astype(F32))[..., None]
    z = lam * dt
    lam_bar = jnp.exp(z)
    ks = jnp.arange(L + 1, dtype=F32)
    pw = jnp.exp(z[..., None] * ks.astype(jnp.complex64))
    b_bar = ((lam_bar - 1.0) / lam)[..., None] * lax.complex(b_re.astype(F32), b_im.astype(F32))
    c_mat = lax.complex(c_re.astype(F32), c_im.astype(F32))
    g, n, p = b_bar.shape[1:]
    kern = jnp.einsum("dgpn,dgnk,dgnq->dgkpq", c_mat, pw[..., :L], b_bar).real

    def causal_toeplitz(k):
        kpad = jnp.concatenate([jnp.zeros((g, L - 1, p, p), F32), k], axis=1)
        return jnp.stack([kpad[:, L - 1 - lp:2 * L - 1 - lp] for lp in range(L)], axis=1)

    diag = jnp.eye(L, dtype=F32)[None, :, :, None, None]
    eye = jnp.eye(p, dtype=F32) * d_skip.astype(F32).reshape(g, 1, 1, p, 1)
    m = causal_toeplitz(kern[0]) + causal_toeplitz(kern[1]).transpose(0, 2, 1, 3, 4) + diag * eye
    m = m.transpose(0, 1, 4, 2, 3).reshape(g, L * p, L * p)

    cf = jnp.einsum("gnl,gnq->glqn", pw[0][..., :L][..., ::-1], b_bar[0])
    cb = jnp.einsum("gnl,gnq->glqn", pw[1][..., :L], b_bar[1])
    cf = cf.reshape(g, L * p, n)
    cb = cb.reshape(g, L * p, n)
    bx = jnp.concatenate([cf.real, cf.imag, cf.imag, cf.real, cb.real, cb.imag, cb.imag, cb.real], axis=-1)

    wf = jnp.einsum("gpn,gnl->gnlp", c_mat[0], pw[0][..., 1:]).reshape(g, n, L * p)
    wb = jnp.einsum("gpn,gnl->gnlp", c_mat[1], pw[1][..., 1:][..., ::-1]).reshape(g, n, L * p)
    cyf = jnp.concatenate([wf.real, -wf.imag], axis=1)
    cyb = jnp.concatenate([wb.real, -wb.imag], axis=1)

    al = pw[..., L]

    def pack(x, y):
        return jnp.concatenate([x, y], axis=-1).reshape(1, g * 2 * n)

    a1f, a2f = pack(al[0].real, al[0].real), pack(-al[0].imag, al[0].imag)
    a1b, a2b = pack(al[1].real, al[1].real), pack(-al[1].imag, al[1].imag)
    return (m.astype(BF16), bx.astype(BF16), cyf.astype(BF16), cyb.astype(BF16),
            a1f.astype(F32), a2f.astype(F32), a1b.astype(F32), a2b.astype(F32))


def _s5x_body(u_ref, bx_ref, xf_ref, xfs_ref, xb_ref, xbs_ref):
    x = jnp.dot(u_ref[0], bx_ref[0], preferred_element_type=F32)
    xf_ref[...] = x[:, 0 * LANES:1 * LANES]
    xfs_ref[...] = x[:, 1 * LANES:2 * LANES]
    xb_ref[...] = x[:, 2 * LANES:3 * LANES]
    xbs_ref[...] = x[:, 3 * LANES:4 * LANES]


def _s5scan_body(xf_ref, xfs_ref, xb_ref, xbs_ref, a1f_ref, a2f_ref, a1b_ref, a2b_ref, hf_ref, hb_ref, *,
                 seq_chunks):
    a1f, a2f, a1b, a2b = a1f_ref[...], a2f_ref[...], a1b_ref[...], a2b_ref[...]
    zero = jnp.zeros_like(a1f)
    for c0, n in seq_chunks:

        def step(i, carry, c0=c0, n=n):
            hf, hfs, hb, hbs = carry
            rf = c0 + i
            rb = c0 + n - 1 - i
            hf_ref[pl.ds(rf, 1), :] = hf
            hb_ref[pl.ds(rb, 1), :] = hb
            xf = xf_ref[pl.ds(rf, 1), :]
            xfs = xfs_ref[pl.ds(rf, 1), :]
            xb = xb_ref[pl.ds(rb, 1), :]
            xbs = xbs_ref[pl.ds(rb, 1), :]
            nhf = a1f * hf + a2f * hfs + xf
            nhfs = a1f * hfs - a2f * hf + xfs
            nhb = a1b * hb + a2b * hbs + xb
            nhbs = a1b * hbs - a2b * hb + xbs
            return nhf, nhfs, nhb, nhbs

        lax.fori_loop(0, n, step, (zero, zero, zero, zero))


def _chunk_permutation():
    L, p = SSM_CHUNK, SSM_GROUP
    gpb = LANES // p
    i = jnp.arange(L * LANES)
    dst = ((i % LANES) // p) * (L * p) + (i // LANES) * p + i % p
    return (dst[:, None] == jnp.arange(L * LANES)[None, :]).astype(BF16), gpb


def _s5_gather_body(*refs):
    u_refs, perm_ref, o_ref = refs[:-2], refs[-2], refs[-1]
    ucat = jnp.concatenate([r[...].astype(BF16) for r in u_refs], axis=1)
    res = jnp.dot(ucat, perm_ref[...], preferred_element_type=F32)
    lp = o_ref.shape[2]
    for g8 in range(o_ref.shape[0]):
        o_ref[g8] = res[:, g8 * lp:(g8 + 1) * lp].astype(o_ref.dtype)


def _s5y_body(u_ref, m_ref, hf_ref, hb_ref, cyf_ref, cyb_ref, permt_ref, y_ref):
    ys = []
    for g8 in range(u_ref.shape[0]):
        ls = slice(g8 * LANES, (g8 + 1) * LANES)
        y = jnp.dot(u_ref[g8], m_ref[g8], preferred_element_type=F32)
        y = y + jnp.dot(hf_ref[:, ls].astype(BF16), cyf_ref[g8], preferred_element_type=F32)
        y = y + jnp.dot(hb_ref[:, ls].astype(BF16), cyb_ref[g8], preferred_element_type=F32)
        ys.append(y)
    ycat = jnp.concatenate(ys, axis=1)
    hi = ycat.astype(BF16)
    lo = (ycat - hi.astype(F32)).astype(BF16)
    res = (jnp.dot(hi, permt_ref[...], preferred_element_type=F32)
           + jnp.dot(lo, permt_ref[...], preferred_element_type=F32))
    for l in range(y_ref.shape[1]):
        y_ref[:, l, :] = res[:, l * LANES:(l + 1) * LANES]


def s5_mix(zs, col0, weights, seqs):
    m, bx, cyf, cyb, a1f, a2f, a1b, a2b = weights
    t, ncols = zs.shape
    g = m.shape[0]
    L, p = SSM_CHUNK, SSM_GROUP
    lp = L * p
    c = t // L
    perm, gpb = _chunk_permutation()
    nblk = g // gpb
    tr = _tile(256, c)
    cpr = ncols // LANES
    zs_chunks = zs.reshape(c, L * ncols)
    ug = pl.pallas_call(
        _s5_gather_body,
        grid=(nblk, c // tr),
        in_specs=[pl.BlockSpec((tr, LANES), functools.partial(lambda b, i, l: (i, l * cpr + col0 // LANES + b), l=l))
                  for l in range(L)] + [pl.BlockSpec((L * LANES, L * LANES), lambda b, i: (0, 0))],
        out_specs=pl.BlockSpec((gpb, tr, lp), lambda b, i: (b, i, 0)),
        out_shape=jax.ShapeDtypeStruct((g, c, lp), BF16),
        compiler_params=_cparams(("parallel", "parallel")),
        name="s5_gather_chunks",
    )(*([zs_chunks] * L), perm)
    xs = pl.pallas_call(
        _s5x_body,
        grid=(g,),
        in_specs=[pl.BlockSpec((1, c, lp), lambda i: (i, 0, 0)),
                  pl.BlockSpec((1, lp, 4 * LANES), lambda i: (i, 0, 0))],
        out_specs=[pl.BlockSpec((c, LANES), lambda i: (0, i))] * 4,
        out_shape=[jax.ShapeDtypeStruct((c, g * LANES), F32)] * 4,
        compiler_params=_cparams(("parallel",)),
        name="s5_chunk_states",
    )(ug, bx)
    seq_chunks = tuple((a // L, (b - a) // L) for a, b in seqs)
    tc = _tile(512, g * LANES)
    xspec = pl.BlockSpec((c, tc), lambda i: (0, i))
    aspec = pl.BlockSpec((1, tc), lambda i: (0, i))
    hf, hb = pl.pallas_call(
        functools.partial(_s5scan_body, seq_chunks=seq_chunks),
        grid=(g * LANES // tc,),
        in_specs=[xspec] * 4 + [aspec] * 4,
        out_specs=[xspec] * 2,
        out_shape=[jax.ShapeDtypeStruct((c, g * LANES), F32)] * 2,
        compiler_params=_cparams(("parallel",)),
        name="s5_state_scan",
    )(*xs, a1f, a2f, a1b, a2b)
    y = pl.pallas_call(
        _s5y_body,
        grid=(nblk, c // tr),
        in_specs=[pl.BlockSpec((gpb, tr, lp), lambda b, i: (b, i, 0)),
                  pl.BlockSpec((gpb, lp, lp), lambda b, i: (b, 0, 0)),
                  pl.BlockSpec((tr, gpb * LANES), lambda b, i: (i, b)),
                  pl.BlockSpec((tr, gpb * LANES), lambda b, i: (i, b)),
                  pl.BlockSpec((gpb, LANES, lp), lambda b, i: (b, 0, 0)),
                  pl.BlockSpec((gpb, LANES, lp), lambda b, i: (b, 0, 0)),
                  pl.BlockSpec((L * LANES, L * LANES), lambda b, i: (0, 0))],
        out_specs=pl.BlockSpec((tr, L, LANES), lambda b, i: (i, 0, b)),
        out_shape=jax.ShapeDtypeStruct((c, L, g * p), F32),
        compiler_params=_cparams(("parallel", "parallel")),
        name="s5_outputs",
    )(ug, m, hf, hb, cyf, cyb, perm.T)
    return y.reshape(t, g * p)


def _glu_body(y_ref, w_ref, o_ref):
    gl = _gelu_tanh(y_ref[...])
    zz = jnp.dot(gl.astype(BF16), w_ref[...], preferred_element_type=F32)
    o_ref[...] = (gl * jax.nn.sigmoid(zz)).astype(o_ref.dtype)


def glu(y, w_glu, tm=512):
    t, w = y.shape
    tm = _tile(tm, t)
    return pl.pallas_call(
        _glu_body,
        grid=(t // tm,),
        in_specs=[pl.BlockSpec((tm, w), lambda i: (i, 0)),
                  pl.BlockSpec((w, w), lambda i: (0, 0))],
        out_specs=pl.BlockSpec((tm, w), lambda i: (i, 0)),
        out_shape=jax.ShapeDtypeStruct((t, w), BF16),
        compiler_params=_cparams(("parallel",)),
        name="s5_glu",
    )(y, w_glu)


def _merge_body(ya_ref, yb_ref, yc_ref, pa_ref, pb_ref, pc_ref, ga_ref, gb_ref, gc_ref, o_ref):
    a = jnp.dot(ya_ref[...], pa_ref[...], preferred_element_type=F32)
    b = jnp.dot(yb_ref[...], pb_ref[...], preferred_element_type=F32)
    c = jnp.dot(yc_ref[...], pc_ref[...], preferred_element_type=F32)
    o = ga_ref[...].astype(F32) * a + gb_ref[...].astype(F32) * b + gc_ref[...].astype(F32) * c
    o_ref[...] = o.astype(o_ref.dtype)


def gated_merge(ya, yb, yc, pa, pb, pc, gates, tm=512, tn=1024):
    t = ya.shape[0]
    d = pa.shape[1]
    tm = _tile(tm, t)
    tn = _tile(tn, d)
    nj = d // tn

    def aspec(x):
        return pl.BlockSpec((tm, x.shape[1]), lambda i, j: (i, 0))

    def wspec(x):
        return pl.BlockSpec((x.shape[0], tn), lambda i, j: (0, j))

    gspecs = [pl.BlockSpec((tm, tn), functools.partial(lambda i, j, k: (i, k * nj + j), k=k)) for k in range(3)]
    return pl.pallas_call(
        _merge_body,
        grid=(t // tm, nj),
        in_specs=[aspec(ya), aspec(yb), aspec(yc), wspec(pa), wspec(pb), wspec(pc)] + gspecs,
        out_specs=pl.BlockSpec((tm, tn), lambda i, j: (i, j)),
        out_shape=jax.ShapeDtypeStruct((t, d), BF16),
        compiler_params=_cparams(("parallel", "parallel")),
        name="gated_merge",
    )(ya, yb, yc, pa, pb, pc, gates, gates, gates)


def _oddeven_merge(lo, hi, r):
    step = r * 2
    if step < hi - lo:
        yield from _oddeven_merge(lo, hi, step)
        yield from _oddeven_merge(lo + r, hi, step)
        yield from [(i, i + r) for i in range(lo + r, hi - r, step)]
    else:
        yield (lo, lo + r)


def _oddeven_merge_sort(lo, hi):
    if hi - lo >= 1:
        mid = lo + (hi - lo) // 2
        yield from _oddeven_merge_sort(lo, mid)
        yield from _oddeven_merge_sort(mid + 1, hi)
        yield from _oddeven_merge(lo, hi, 1)


_SORT16 = tuple(_oddeven_merge_sort(0, PEER_TOPK - 1))
_BITONIC16 = tuple((i, i + d) for d in (8, 4, 2, 1) for i in range(PEER_TOPK) if not i & d)


def _compare_exchange(v, pairs):
    for i, j in pairs:
        hi = jnp.maximum(v[i], v[j])
        lo = jnp.minimum(v[i], v[j])
        v[i], v[j] = hi, lo


def _top16_desc(v):
    v = list(v)
    _compare_exchange(v, _SORT16)
    for shift in (4, 2, 1):
        r = [pltpu.roll(x, shift, axis=0) for x in v]
        v = [jnp.maximum(v[i], r[PEER_TOPK - 1 - i]) for i in range(PEER_TOPK)]
        _compare_exchange(v, _BITONIC16)
    return v


def _peer_select_body(q_ref, keys_ref, s1_ref, s2_ref, e2_ref, tau_ref, m1_ref, *, heads):
    tt = q_ref.shape[0]
    sub = lax.broadcasted_iota(jnp.int32, (SUBLANES, tt), 0)
    ninf = jnp.full((SUBLANES, tt), -jnp.inf, F32)

    def by_sublane(lst):
        out = lst[SUBLANES - 1]
        for j in range(SUBLANES - 2, -1, -1):
            out = jnp.where(sub == j, lst[j], out)
        return out

    for h in range(heads):
        scores, tops = [], []
        for c in range(2):
            blk = 2 * h + c
            qb = q_ref[:, blk * N_KEYS:(blk + 1) * N_KEYS]
            kb = keys_ref[blk]
            q_hi, k_hi = qb.astype(BF16), kb.astype(BF16)
            q_lo = (qb - q_hi.astype(F32)).astype(BF16)
            k_lo = (kb - k_hi.astype(F32)).astype(BF16)
            nt = (((1,), (1,)), ((), ()))
            st = (lax.dot_general(k_hi, q_hi, nt, preferred_element_type=F32)
                  + lax.dot_general(k_hi, q_lo, nt, preferred_element_type=F32)
                  + lax.dot_general(k_lo, q_hi, nt, preferred_element_type=F32))
            scores.append(st)
            tops.append(_top16_desc([st[SUBLANES * a:SUBLANES * (a + 1), :] for a in range(N_KEYS // SUBLANES)]))
        t1, t2 = tops
        t2a, t2b, t1b = by_sublane(t2[:8]), by_sublane(t2[8:]), by_sublane(t1[8:])
        cand = [t1[0] + t2a, t1[0] + t2b, t1[1] + t2a,
                jnp.where(sub < 5, t1[2] + t2a, ninf), jnp.where(sub < 4, t1[3] + t2a, ninf),
                jnp.where(sub < 3, t1[4] + t2a, ninf), jnp.where(sub < 2, t1[5] + t2a, ninf),
                jnp.where(sub < 2, t1[6] + t2a, ninf), jnp.where(sub < 2, t1[7] + t2a, ninf),
                t1b + t2[0]] + [ninf] * 6
        best = _top16_desc(cand)
        z = jnp.exp(best[0] - best[0])
        for b in best[1:]:
            z = z + jnp.exp(b - best[0])
        s1_ref[h] = scores[0]
        s2_ref[h] = scores[1]
        e2_ref[h] = jnp.exp(scores[1] - t2[0][0:1, :]) / z[0:1, :]
        tau_ref[h:h + 1, :] = best[PEER_TOPK - 1][0:1, :]
        m1_ref[h:h + 1, :] = t1[0][0:1, :]


def peer_select(q, keys, heads, tt=256):
    t = q.shape[0]
    tt = _tile(tt, t)
    big = jax.ShapeDtypeStruct((heads, N_KEYS, t), F32)
    small = jax.ShapeDtypeStruct((heads, t), F32)
    bspec = pl.BlockSpec((heads, N_KEYS, tt), lambda i: (0, 0, i))
    sspec = pl.BlockSpec((heads, tt), lambda i: (0, i))
    return pl.pallas_call(
        functools.partial(_peer_select_body, heads=heads),
        grid=(t // tt,),
        in_specs=[pl.BlockSpec((tt, q.shape[1]), lambda i: (i, 0)),
                  pl.BlockSpec(keys.shape, lambda i: (0, 0, 0))],
        out_specs=[bspec, bspec, bspec, sspec, sspec],
        out_shape=[big, big, big, small, small],
        compiler_params=_cparams(("parallel",)),
        name="peer_select",
    )(q, keys)


def _peer_dense_body(x_ref, h_ref, u_ref, v_ref, sc_ref, s1_ref, s2_ref, e2_ref, tau_ref, m1_ref, o_ref, *, heads):
    j = pl.program_id(1)
    te = u_ref.shape[0]
    na = te // N_KEYS

    @pl.when(j == 0)
    def _():
        o_ref[...] = x_ref[...]

    act = lax.dot_general(u_ref[...], h_ref[...], (((1,), (1,)), ((), ())), preferred_element_type=F32)
    act = act * sc_ref[:, 0:1]
    rows = []
    for al in range(na):
        a = j * na + al
        g = None
        for h in range(heads):
            s1row = s1_ref[h, pl.ds(a, 1), :]
            e1row = jnp.exp(s1row - m1_ref[h:h + 1, :]).astype(BF16)
            sel = (s1row + s2_ref[h]) >= tau_ref[h:h + 1, :]
            contrib = jnp.where(sel, e1row * e2_ref[h].astype(BF16), jnp.zeros((), BF16))
            g = contrib if g is None else g + contrib
        rows.append(g)
    gate = rows[0] if na == 1 else jnp.concatenate(rows, axis=0)
    wgt = gate.astype(F32) * _gelu_tanh(act)
    peak = jnp.max(jnp.abs(wgt), axis=(0, 1), keepdims=True)
    ws = jnp.where(peak > 0.0, jnp.exp2(jnp.floor(jnp.log2(F8_TARGET / jnp.where(peak > 0.0, peak, 1.0)))), 1.0)
    part = lax.dot_general((wgt * ws).astype(F8), v_ref[...], (((0,), (0,)), ((), ())),
                           preferred_element_type=F32)
    o_ref[...] += part * (sc_ref[:, 1:2] / ws)


def peer_dense(x, h, u_tab, v_tab, layer, scales, s1, s2, e2, tau, m1, heads, tm=512, te=1024):
    t, d = h.shape
    e = u_tab.shape[1]
    tm = _tile(tm, t)
    te = _tile(te, e)
    once = pl.Buffered(1)
    bspec = pl.BlockSpec((heads, N_KEYS, tm), lambda i, j: (0, 0, i), pipeline_mode=once)
    sspec = pl.BlockSpec((heads, tm), lambda i, j: (0, i))
    return pl.pallas_call(
        functools.partial(_peer_dense_body, heads=heads),
        grid=(t // tm, e // te),
        in_specs=[pl.BlockSpec((tm, d), lambda i, j: (i, 0), pipeline_mode=once),
                  pl.BlockSpec((tm, d), lambda i, j: (i, 0), pipeline_mode=once),
                  pl.BlockSpec((None, te, d), lambda i, j: (layer, j, 0)),
                  pl.BlockSpec((None, te, d), lambda i, j: (layer, j, 0)),
                  pl.BlockSpec((1, 2), lambda i, j: (0, 0)),
                  bspec, bspec, bspec, sspec, sspec],
        out_specs=pl.BlockSpec((tm, d), lambda i, j: (i, 0)),
        out_shape=jax.ShapeDtypeStruct((t, d), F32),
        compiler_params=_cparams(("parallel", "arbitrary")),
        name="peer_dense",
    )(x, h, u_tab, v_tab, scales.astype(F32), s1, s2, e2, tau, m1)


def _rope_table(seqs):
    inv = 1.0 / (ROPE_BASE ** (jnp.arange(0, QK_ROPE, 2, dtype=F32) / QK_ROPE))
    pos = jnp.concatenate([jnp.arange(b - a, dtype=F32) for a, b in seqs])
    ang = pos[:, None] * inv[None, :]
    c, s = jnp.cos(ang), jnp.sin(ang)
    return jnp.concatenate([c, c, s, s], axis=-1)


def _rotate_half_cols(w):
    half = w.shape[-1] // 2
    return jnp.concatenate([-w[..., half:], w[..., :half]], axis=-1)


def _mixer(x, seqs, cs, norm_mix, w_in, pool_w, pool_scale, q_norm, w_uq, kv_norm, w_ukv,
           a_re, a_im, log_dt, b_re, b_im, c_re, c_im, d_skip, w_glu, p_a, p_b, p_c, w_o):
    t, d = x.shape
    pool_width = pool_w.shape[0] * pool_w.shape[1]
    q_lora = w_uq.shape[0]
    kv_lora = w_ukv.shape[0]
    heads = w_uq.shape[1]
    ssm_width = d_skip.shape[0]
    o1 = pool_width
    o2 = o1 + q_lora
    o3 = o2 + kv_lora
    o4 = o3 + QK_ROPE
    o5 = o4 + ssm_width

    w_kpe = w_in[:, o3:o4]
    small_cols = [w_in[:, :o1], w_in[:, o1:o2], w_in[:, o4:o5], w_in[:, o2:o3], w_kpe, _rotate_half_cols(w_kpe)]
    n_small = o5 + QK_ROPE
    tn_small = 1280
    n_pad = -n_small % tn_small
    w_small = jnp.concatenate(small_cols + [jnp.zeros((d, n_pad), w_in.dtype)], axis=1).astype(BF16)
    hn, hn8, s_h = rmsnorm(x, norm_mix, BF16, with_f8=True)
    zs = matmul(hn, w_small, F32, tm=512, tn=tn_small, name="in_proj_small")
    wg8, s_w = _to_f8(w_in[:, o5:])
    gates = matmul(hn8, wg8, BF16, tm=1024, tn=1024, act="sigmoid", out_scale=1.0 / (s_h * s_w),
                   name="in_proj_gates")
    cq_block = o1 // q_lora
    ssm_block = o2 // ssm_width
    ckv_block = (o2 + ssm_width) // kv_lora
    kpe_block = (o2 + ssm_width + kv_lora) // LANES
    assert o1 % q_lora == 0 and o2 % ssm_width == 0 and (o2 + ssm_width) % kv_lora == 0

    ya = pool_mixer(zs, pool_w, pool_scale, seqs)

    wq_nope = w_uq[:, :, :QK_NOPE]
    wq_pe = w_uq[:, :, QK_NOPE:]
    wq = jnp.concatenate([wq_nope, wq_pe, _rotate_half_cols(wq_pe)], axis=-1).reshape(q_lora, heads * HEAD_PAD)
    qp = q_proj(zs, cq_block, q_norm, wq.astype(BF16), cs, heads)
    wk = w_ukv[:, :, :QK_NOPE].reshape(kv_lora, heads * QK_NOPE).astype(BF16)
    wvt = w_ukv[:, :, QK_NOPE:].reshape(kv_lora, heads * V_HEAD).T.astype(BF16)
    kp, v = kv_proj(zs, ckv_block, kpe_block, kv_norm, wk, wvt, cs, heads)
    groups = []
    for a, b in seqs:
        if groups and groups[-1][2] == b - a and groups[-1][0] + groups[-1][1] * groups[-1][2] == a:
            groups[-1][1] += 1
        else:
            groups.append([a, 1, b - a])
    outs = [attention(qp, kp, v, row0, nseq, seq, heads) for row0, nseq, seq in groups]
    yb = outs[0] if len(outs) == 1 else jnp.concatenate(outs, axis=0)

    s5w = _s5_weights(a_re, a_im, log_dt, b_re, b_im, c_re, c_im, d_skip)
    yc = glu(s5_mix(zs, ssm_block * ssm_width, s5w, seqs), w_glu.astype(BF16))

    merged = gated_merge(ya, yb, yc, p_a.astype(BF16), p_b.astype(BF16), p_c.astype(BF16), gates)
    return matmul(merged, w_o.astype(BF16), F32, tm=1024, tn=512, residual=x, name="out_proj")


def _peer(x, layer, norm_ffn, peer_wq, peer_keys, u_tabs, u_scales, v_tabs, v_scales):
    h2, h8, s_h = rmsnorm(x, norm_ffn, BF16, with_f8=True)
    pheads = peer_keys.shape[0]
    q = matmul(h2, peer_wq.astype(BF16), F32, tm=1024, tn=1024, name="peer_query")
    keys = peer_keys.reshape(pheads * 2, N_KEYS, peer_keys.shape[-1]).astype(F32)
    s1, s2, e2, tau, m1 = peer_select(q, keys, pheads)
    scales = jnp.stack([1.0 / (s_h * u_scales[layer]), 1.0 / v_scales[layer]]).reshape(1, 2)
    return peer_dense(x, h8, u_tabs, v_tabs, layer, scales, s1, s2, e2, tau, m1, pheads)


def kernel(x_prompt, x_sample, norm_mix, w_in, pool_w, pool_scale, q_norm, w_uq, kv_norm, w_ukv, ssm_a_re, ssm_a_im, ssm_log_dt, ssm_b_re, ssm_b_im, ssm_c_re, ssm_c_im, ssm_d, w_glu, p_a, p_b, p_c, w_o, norm_ffn, peer_wq, peer_keys, peer_u, peer_v, final_norm):
    d = x_prompt.shape[-1]
    seqs = []
    row = 0
    for xs in (x_prompt, x_sample):
        for _ in range(xs.shape[0]):
            seqs.append((row, row + xs.shape[1]))
            row += xs.shape[1]
    x = jnp.concatenate([x_prompt.reshape(-1, d), x_sample.reshape(-1, d)], axis=0)
    cs = _rope_table(seqs)
    per_layer = (norm_mix, w_in.astype(BF16), pool_w, pool_scale, q_norm, w_uq, kv_norm, w_ukv, ssm_a_re, ssm_a_im, ssm_log_dt,
                 ssm_b_re, ssm_b_im, ssm_c_re, ssm_c_im, ssm_d, w_glu, p_a, p_b, p_c, w_o, norm_ffn, peer_wq,
                 peer_keys)
    n_mix = 21
    u_scales = _pow2_scale(jnp.max(jnp.abs(peer_u), axis=(1, 2)))
    v_scales = _pow2_scale(jnp.max(jnp.abs(peer_v), axis=(1, 2)))
    u_tabs = (peer_u * u_scales[:, None, None]).astype(F8)
    v_tabs = (peer_v * v_scales[:, None, None]).astype(F8)
    for layer in range(norm_mix.shape[0]):
        lw = [w[layer] for w in per_layer]
        x = _mixer(x, seqs, cs, *lw[:n_mix])
        x = _peer(x, layer, *lw[n_mix:], u_tabs, u_scales, v_tabs, v_scales)
    n_prompt = x_prompt.shape[0] * x_prompt.shape[1]
    y_prompt = rmsnorm(x, final_norm, x_prompt.dtype, row0=0, nrows=n_prompt)
    y_sample = rmsnorm(x, final_norm, x_sample.dtype, row0=n_prompt, nrows=x.shape[0] - n_prompt)
    return (y_prompt.reshape(x_prompt.shape), y_sample.reshape(x_sample.shape))
```

```python
import functools
import math

import jax
import jax.numpy as jnp
from jax import lax
from jax.experimental import pallas as pl
from jax.experimental.pallas import tpu as pltpu

F32 = jnp.float32
BF16 = jnp.bfloat16
F8 = jnp.float8_e4m3fn
F8_TARGET = 240.0

EPS = 1e-6
POOL_WINDOWS = (2, 4, 8, 16)
QK_NOPE = 128
QK_ROPE = 64
V_HEAD = 128
ROPE_BASE = 10000.0
SSM_GROUP = 16
SSM_STATE = 64
PEER_TOPK = 16
N_KEYS = 128

LANES = 128
SUBLANES = 8
HEAD_PAD = 256
SSM_CHUNK = 16
VMEM_LIMIT = 56 * 1024 * 1024
POOL_HALO = 8


def _cparams(sem, flags=None):
    return pltpu.CompilerParams(dimension_semantics=sem, vmem_limit_bytes=VMEM_LIMIT, flags=flags)


def _tile(pref, dim):
    t = min(pref, dim)
    assert dim % t == 0, (pref, dim)
    return t


def _gelu_tanh(x):
    cdf = 0.5 * (1.0 + jnp.tanh(math.sqrt(2.0 / math.pi) * (x + 0.044715 * (x * x * x))))
    return x * cdf


def _rms(x, g):
    y = x * lax.rsqrt(jnp.mean(x * x, axis=-1, keepdims=True) + EPS)
    return y * g


def _pow2_scale(bound):
    safe = jnp.where(bound > 0, bound, 1.0).astype(F32)
    return jnp.where(bound > 0, jnp.exp2(jnp.floor(jnp.log2(F8_TARGET / safe))), 1.0).astype(F32)


def _to_f8(w):
    s = _pow2_scale(jnp.max(jnp.abs(w)))
    return (w.astype(F32) * s).astype(F8), s


def _rmsnorm_body(x_ref, g_ref, o_ref):
    o_ref[...] = _rms(x_ref[...].astype(F32), g_ref[...]).astype(o_ref.dtype)


def _rmsnorm_f8_body(x_ref, g_ref, s_ref, o_ref, o8_ref):
    y = _rms(x_ref[...].astype(F32), g_ref[...])
    o_ref[...] = y.astype(o_ref.dtype)
    o8_ref[...] = (y * s_ref[...]).astype(o8_ref.dtype)


def rmsnorm(x, g, out_dtype, tm=512, row0=0, nrows=None, with_f8=False):
    d = x.shape[1]
    t = x.shape[0] if nrows is None else nrows
    tm = _tile(tm, math.gcd(t, row0) if row0 else t)
    b0 = row0 // tm
    xspec = pl.BlockSpec((tm, d), lambda i: (b0 + i, 0))
    gspec = pl.BlockSpec((1, d), lambda i: (0, 0))
    ospec = pl.BlockSpec((tm, d), lambda i: (i, 0))
    g2 = g.reshape(1, d).astype(F32)
    if not with_f8:
        return pl.pallas_call(
            _rmsnorm_body,
            grid=(t // tm,),
            in_specs=[xspec, gspec],
            out_specs=ospec,
            out_shape=jax.ShapeDtypeStruct((t, d), out_dtype),
            compiler_params=_cparams(("parallel",)),
            name="rmsnorm",
        )(x, g2)
    s = _pow2_scale(math.sqrt(d) * jnp.max(jnp.abs(g2)))
    y, y8 = pl.pallas_call(
        _rmsnorm_f8_body,
        grid=(t // tm,),
        in_specs=[xspec, gspec, pl.BlockSpec((1, 1), lambda i: (0, 0))],
        out_specs=[ospec, ospec],
        out_shape=[jax.ShapeDtypeStruct((t, d), out_dtype), jax.ShapeDtypeStruct((t, d), F8)],
        compiler_params=_cparams(("parallel",)),
        name="rmsnorm_f8",
    )(x, g2, s.reshape(1, 1))
    return y, y8, s


def _mm_body(a_ref, b_ref, o_ref, *, act):
    acc = jnp.dot(a_ref[...], b_ref[...], preferred_element_type=F32)
    if act == "sigmoid":
        acc = jax.nn.sigmoid(acc)
    o_ref[...] = acc.astype(o_ref.dtype)


def _mm_scaled_body(a_ref, b_ref, s_ref, o_ref, *, act):
    acc = jnp.dot(a_ref[...], b_ref[...], preferred_element_type=F32) * s_ref[...]
    if act == "sigmoid":
        acc = jax.nn.sigmoid(acc)
    o_ref[...] = acc.astype(o_ref.dtype)


def _mm_res_body(a_ref, b_ref, r_ref, o_ref):
    acc = jnp.dot(a_ref[...], b_ref[...], preferred_element_type=F32)
    o_ref[...] = (r_ref[...] + acc).astype(o_ref.dtype)


def matmul(a, b, out_dtype, tm, tn, act=None, residual=None, out_scale=None, name="matmul"):
    m, k = a.shape
    n = b.shape[1]
    tm = _tile(tm, m)
    tn = _tile(tn, n)
    in_specs = [pl.BlockSpec((tm, k), lambda i, j: (i, 0)),
                pl.BlockSpec((k, tn), lambda i, j: (0, j))]
    args = [a, b]
    if out_scale is not None:
        body = functools.partial(_mm_scaled_body, act=act)
        in_specs.append(pl.BlockSpec((1, 1), lambda i, j: (0, 0)))
        args.append(out_scale.reshape(1, 1).astype(F32))
    elif residual is None:
        body = functools.partial(_mm_body, act=act)
    else:
        body = _mm_res_body
        in_specs.append(pl.BlockSpec((tm, tn), lambda i, j: (i, j)))
        args.append(residual)
    return pl.pallas_call(
        body,
        grid=(m // tm, n // tn),
        in_specs=in_specs,
        out_specs=pl.BlockSpec((tm, tn), lambda i, j: (i, j)),
        out_shape=jax.ShapeDtypeStruct((m, n), out_dtype),
        compiler_params=_cparams(("parallel", "parallel")),
        name=name,
    )(*args)


def _seq_bounds(r0, seqs):
    s0 = jnp.int32(0)
    s1 = jnp.int32(0)
    for a, b in seqs:
        inside = jnp.logical_and(r0 >= a, r0 < b)
        s0 = jnp.where(inside, jnp.int32(a), s0)
        s1 = jnp.where(inside, jnp.int32(b), s1)
    return s0, s1


def _pool_body(prev_ref, cur_ref, next_ref, w_ref, sc_ref, o_ref, *, ts, seqs, group):
    r0 = pl.program_id(0) * ts
    s0, s1 = _seq_bounds(r0, seqs)
    cur = cur_ref[...].astype(F32)
    prev = jnp.where(r0 > s0, prev_ref[...].astype(F32), 0.0)
    nxt = jnp.where(r0 + ts < s1, next_ref[...].astype(F32), 0.0)
    ext = jnp.concatenate([prev, cur, nxt], axis=0).astype(BF16)
    te = ts + 2 * POOL_HALO
    d = lax.broadcasted_iota(jnp.int32, (ts, te), 1) - lax.broadcasted_iota(jnp.int32, (ts, te), 0)
    rows = r0 + lax.broadcasted_iota(jnp.int32, (ts, group), 0)
    for gi, w in enumerate(POOL_WINDOWS):
        left = w // 2
        right = w - 1 - left
        band = jnp.where(d >= POOL_HALO - left, jnp.where(d <= POOL_HALO + right, 1.0, 0.0), 0.0).astype(BF16)
        sl = slice(gi * group, (gi + 1) * group)
        tot = jnp.dot(band, ext[:, sl], preferred_element_type=F32)
        lo = jnp.maximum(rows - left, s0)
        hi = jnp.minimum(rows + right, s1 - 1)
        cnt = (hi - lo + 1).astype(F32)
        p = tot / cnt - cur[:, sl]
        y = jnp.dot(p.astype(BF16), w_ref[gi], preferred_element_type=F32) * sc_ref[:, sl]
        o_ref[:, sl] = y.astype(o_ref.dtype)


def pool_mixer(zs, pool_w, pool_scale, seqs, ts=512):
    t = zs.shape[0]
    ng, group, _ = pool_w.shape
    width = ng * group
    ts = _tile(ts, min(b - a for a, b in seqs))
    hb = ts // POOL_HALO
    nrow8 = t // POOL_HALO
    body = functools.partial(_pool_body, ts=ts, seqs=tuple(seqs), group=group)
    return pl.pallas_call(
        body,
        grid=(t // ts,),
        in_specs=[pl.BlockSpec((POOL_HALO, width), lambda i: (jnp.maximum(i * hb - 1, 0), 0)),
                  pl.BlockSpec((ts, width), lambda i: (i, 0)),
                  pl.BlockSpec((POOL_HALO, width), lambda i: (jnp.minimum((i + 1) * hb, nrow8 - 1), 0)),
                  pl.BlockSpec((ng, group, group), lambda i: (0, 0, 0)),
                  pl.BlockSpec((1, width), lambda i: (0, 0))],
        out_specs=pl.BlockSpec((ts, width), lambda i: (i, 0)),
        out_shape=jax.ShapeDtypeStruct((t, width), BF16),
        compiler_params=_cparams(("parallel",)),
        name="pool_mixer",
    )(zs, zs, zs, pool_w.astype(BF16), pool_scale.reshape(1, width).astype(F32))


def _rope_combine(x, cs):
    tmp = x * cs
    return tmp + pltpu.roll(tmp, LANES // 2, axis=1)


def _qproj_body(z_ref, g_ref, w_ref, cs_ref, o_ref, *, heads, scale):
    cn = _rms(z_ref[...].astype(F32), g_ref[...]).astype(BF16)
    acc = jnp.dot(cn, w_ref[...], preferred_element_type=F32)
    cs = cs_ref[...] * scale
    for h in range(heads):
        lo = acc[:, h * HEAD_PAD:h * HEAD_PAD + LANES] * scale
        hi = _rope_combine(acc[:, h * HEAD_PAD + LANES:(h + 1) * HEAD_PAD], cs)
        o_ref[:, h * HEAD_PAD:h * HEAD_PAD + LANES] = lo.astype(o_ref.dtype)
        o_ref[:, h * HEAD_PAD + LANES:(h + 1) * HEAD_PAD] = hi.astype(o_ref.dtype)


def q_proj(zs, col_block, q_norm, wq, cs, heads, tm=512):
    t = zs.shape[0]
    r = wq.shape[0]
    tm = _tile(tm, t)
    scale = (QK_NOPE + QK_ROPE) ** -0.5 * math.log2(math.e)
    body = functools.partial(_qproj_body, heads=heads, scale=scale)
    return pl.pallas_call(
        body,
        grid=(t // tm,),
        in_specs=[pl.BlockSpec((tm, r), lambda i: (i, col_block)),
                  pl.BlockSpec((1, r), lambda i: (0, 0)),
                  pl.BlockSpec((r, heads * HEAD_PAD), lambda i: (0, 0)),
                  pl.BlockSpec((tm, LANES), lambda i: (i, 0))],
        out_specs=pl.BlockSpec((tm, heads * HEAD_PAD), lambda i: (i, 0)),
        out_shape=jax.ShapeDtypeStruct((t, heads * HEAD_PAD), BF16),
        compiler_params=_cparams(("parallel",)),
        name="q_proj",
    )(zs, q_norm.reshape(1, r).astype(F32), wq, cs)


def _kvproj_body(z_ref, g_ref, kpe_ref, wk_ref, wvt_ref, cs_ref, k_ref, vt_ref, *, heads):
    cn = _rms(z_ref[...].astype(F32), g_ref[...]).astype(BF16)
    kn = jnp.dot(cn, wk_ref[...], preferred_element_type=F32)
    vt = lax.dot_general(wvt_ref[...], cn, (((1,), (1,)), ((), ())), preferred_element_type=F32)
    kr = _rope_combine(kpe_ref[...].astype(F32), cs_ref[...])
    lane = lax.broadcasted_iota(jnp.int32, kr.shape, 1)
    kr = jnp.where(lane < QK_ROPE, kr, 0.0).astype(k_ref.dtype)
    for h in range(heads):
        k_ref[:, h * HEAD_PAD:h * HEAD_PAD + LANES] = kn[:, h * LANES:(h + 1) * LANES].astype(k_ref.dtype)
        k_ref[:, h * HEAD_PAD + LANES:(h + 1) * HEAD_PAD] = kr
    vt_ref[...] = vt.astype(vt_ref.dtype)


def kv_proj(zs, ckv_block, kpe_block, kv_norm, wk, wvt, cs, heads, tm=512):
    t = zs.shape[0]
    r = wk.shape[0]
    tm = _tile(tm, t)
    body = functools.partial(_kvproj_body, heads=heads)
    return pl.pallas_call(
        body,
        grid=(t // tm,),
        in_specs=[pl.BlockSpec((tm, r), lambda i: (i, ckv_block)),
                  pl.BlockSpec((1, r), lambda i: (0, 0)),
                  pl.BlockSpec((tm, LANES), lambda i: (i, kpe_block)),
                  pl.BlockSpec((r, heads * QK_NOPE), lambda i: (0, 0)),
                  pl.BlockSpec((heads * V_HEAD, r), lambda i: (0, 0)),
                  pl.BlockSpec((tm, LANES), lambda i: (i, 0))],
        out_specs=[pl.BlockSpec((tm, heads * HEAD_PAD), lambda i: (i, 0)),
                   pl.BlockSpec((heads * V_HEAD, tm), lambda i: (0, i))],
        out_shape=[jax.ShapeDtypeStruct((t, heads * HEAD_PAD), BF16),
                   jax.ShapeDtypeStruct((heads * V_HEAD, t), BF16)],
        compiler_params=_cparams(("parallel",)),
        name="kv_proj",
    )(zs, kv_norm.reshape(1, r).astype(F32), zs, wk, wvt, cs)


def _sublane_allreduce(x, op):
    for shift in (4, 2, 1):
        x = op(x, pltpu.roll(x, shift, axis=0))
    return x


def _attn_body(q_ref, k_ref, vt_ref, o_ref, s_scr, smax_scr, *, tk, nsplit):
    tq = q_ref.shape[0]
    tw = tq // nsplit
    nk = k_ref.shape[0] // tk
    vr = V_HEAD // SUBLANES
    qt = q_ref[...].astype(F32).T.astype(BF16)
    qts = [qt[:, i * tw:(i + 1) * tw] for i in range(nsplit)]

    def produce(j, slot):
        ks = k_ref[pl.ds(pl.multiple_of(j * tk, tk), tk), :]
        for i, qh in enumerate(qts):
            s = jnp.dot(ks, qh, preferred_element_type=F32).reshape(tk // SUBLANES, SUBLANES, tw)
            s_scr[slot, i] = s
            smax_scr[slot, i] = jnp.max(s, axis=0)

    def consume(j, slot, state):
        vts = vt_ref[:, pl.ds(pl.multiple_of(j * tk, tk), tk)]
        new = []
        for i, (m, l, acc) in enumerate(state):
            mn = jnp.maximum(m, _sublane_allreduce(smax_scr[slot, i], jnp.maximum))
            p = jnp.exp2(s_scr[slot, i] - mn[None])
            alpha = jnp.exp2(m - mn)
            l = alpha * l + jnp.sum(p, axis=0)
            pv = jnp.dot(vts, p.reshape(tk, tw).astype(BF16), preferred_element_type=F32)
            acc = alpha[None] * acc + pv.reshape(vr, SUBLANES, tw)
            new.append((mn, l, acc))
        return tuple(new)

    def pair(jj, state):
        j = 2 * jj
        produce(j + 1, 1)
        state = consume(j, 0, state)
        produce(j + 2, 0)
        return consume(j + 1, 1, state)

    state = tuple((jnp.full((SUBLANES, tw), -jnp.inf, F32), jnp.zeros((SUBLANES, tw), F32),
                   jnp.zeros((vr, SUBLANES, tw), F32)) for _ in range(nsplit))
    produce(0, 0)
    if nk == 1:
        fin = consume(0, 0, state)
    else:
        assert nk % 2 == 0
        state = lax.fori_loop(0, nk // 2 - 1, pair, state)
        produce(nk - 1, 1)
        fin = consume(nk - 1, 1, consume(nk - 2, 0, state))
    for i, (_, l, acc) in enumerate(fin):
        lt = _sublane_allreduce(l, jnp.add)
        o = (acc / lt[None]).reshape(V_HEAD, tw)
        o_ref[i * tw:(i + 1) * tw, :] = o.T.astype(o_ref.dtype)


def attention(qp, kp, vt, row0, nseq, seq, heads, tq=1024, tk=1024, nsplit=2):
    tq = _tile(tq, seq)
    tk = _tile(tk, seq)
    nq = seq // tq
    q0 = row0 // tq
    b0 = row0 // seq
    assert row0 % seq == 0
    return pl.pallas_call(
        functools.partial(_attn_body, tk=tk, nsplit=nsplit),
        grid=(nseq, heads, nq),
        in_specs=[pl.BlockSpec((tq, HEAD_PAD), lambda b, h, i: (q0 + b * nq + i, h)),
                  pl.BlockSpec((seq, HEAD_PAD), lambda b, h, i: (b0 + b, h)),
                  pl.BlockSpec((V_HEAD, seq), lambda b, h, i: (h, b0 + b))],
        out_specs=pl.BlockSpec((tq, V_HEAD), lambda b, h, i: (b * nq + i, h)),
        out_shape=jax.ShapeDtypeStruct((nseq * seq, heads * V_HEAD), BF16),
        scratch_shapes=[pltpu.VMEM((2, nsplit, tk // SUBLANES, SUBLANES, tq // nsplit), F32),
                        pltpu.VMEM((2, nsplit, SUBLANES, tq // nsplit), F32)],
        compiler_params=_cparams(("parallel", "parallel", "parallel")),
        name="mla_attention",
    )(qp, kp, vt)


def _s5_weights(a_re, a_im, log_dt, b_re, b_im, c_re, c_im, d_skip):
    L = SSM_CHUNK
    lam = lax.complex(a_re.astype(F32), a_im.astype(F32))
    dt = jnp.exp(log_dt.astype(F32))[..., None]
    z = lam * dt
    lam_bar = jnp.exp(z)
    ks = jnp.arange(L + 1, dtype=F32)
    pw = jnp.exp(z[..., None] * ks.astype(jnp.complex64))
    b_bar = ((lam_bar - 1.0) / lam)[..., None] * lax.complex(b_re.astype(F32), b_im.astype(F32))
    c_mat = lax.complex(c_re.astype(F32), c_im.astype(F32))
    g, n, p = b_bar.shape[1:]
    kern = jnp.einsum("dgpn,dgnk,dgnq->dgkpq", c_mat, pw[..., :L], b_bar).real

    def causal_toeplitz(k):
        kpad = jnp.concatenate([jnp.zeros((g, L - 1, p, p), F32), k], axis=1)
        return jnp.stack([kpad[:, L - 1 - lp:2 * L - 1 - lp] for lp in range(L)], axis=1)

    diag = jnp.eye(L, dtype=F32)[None, :, :, None, None]
    eye = jnp.eye(p, dtype=F32) * d_skip.astype(F32).reshape(g, 1, 1, p, 1)
    m = causal_toeplitz(kern[0]) + causal_toeplitz(kern[1]).transpose(0, 2, 1, 3, 4) + diag * eye
    m = m.transpose(0, 1, 4, 2, 3).reshape(g, L * p, L * p)

    cf = jnp.einsum("gnl,gnq->glqn", pw[0][..., :L][..., ::-1], b_bar[0])
    cb = jnp.einsum("gnl,gnq->glqn", pw[1][..., :L], b_bar[1])
    cf = cf.reshape(g, L * p, n)
    cb = cb.reshape(g, L * p, n)
    bx = jnp.concatenate([cf.real, cf.imag, cf.imag, cf.real, cb.real, cb.imag, cb.imag, cb.real], axis=-1)

    wf = jnp.einsum("gpn,gnl->gnlp", c_mat[0], pw[0][..., 1:]).reshape(g, n, L * p)
    wb = jnp.einsum("gpn,gnl->gnlp", c_mat[1], pw[1][..., 1:][..., ::-1]).reshape(g, n, L * p)
    cyf = jnp.concatenate([wf.real, -wf.imag], axis=1)
    cyb = jnp.concatenate([wb.real, -wb.imag], axis=1)

    al = pw[..., L]

    def pack(x, y):
        return jnp.concatenate([x, y], axis=-1).reshape(1, g * 2 * n)

    a1f, a2f = pack(al[0].real, al[0].real), pack(-al[0].imag, al[0].imag)
    a1b, a2b = pack(al[1].real, al[1].real), pack(-al[1].imag, al[1].imag)
    return (m.astype(BF16), bx.astype(BF16), cyf.astype(BF16), cyb.astype(BF16),
            a1f.astype(F32), a2f.astype(F32), a1b.astype(F32), a2b.astype(F32))


def _s5x_body(u_ref, bx_ref, xf_ref, xfs_ref, xb_ref, xbs_ref):
    x = jnp.dot(u_ref[0], bx_ref[0], preferred_element_type=F32)
    xf_ref[...] = x[:, 0 * LANES:1 * LANES]
    xfs_ref[...] = x[:, 1 * LANES:2 * LANES]
    xb_ref[...] = x[:, 2 * LANES:3 * LANES]
    xbs_ref[...] = x[:, 3 * LANES:4 * LANES]


def _s5scan_body(xf_ref, xfs_ref, xb_ref, xbs_ref, a1f_ref, a2f_ref, a1b_ref, a2b_ref, hf_ref, hb_ref, *,
                 seq_chunks):
    a1f, a2f, a1b, a2b = a1f_ref[...], a2f_ref[...], a1b_ref[...], a2b_ref[...]
    zero = jnp.zeros_like(a1f)
    for c0, n in seq_chunks:

        def step(i, carry, c0=c0, n=n):
            hf, hfs, hb, hbs = carry
            rf = c0 + i
            rb = c0 + n - 1 - i
            hf_ref[pl.ds(rf, 1), :] = hf
            hb_ref[pl.ds(rb, 1), :] = hb
            xf = xf_ref[pl.ds(rf, 1), :]
            xfs = xfs_ref[pl.ds(rf, 1), :]
            xb = xb_ref[pl.ds(rb, 1), :]
            xbs = xbs_ref[pl.ds(rb, 1), :]
            nhf = a1f * hf + a2f * hfs + xf
            nhfs = a1f * hfs - a2f * hf + xfs
            nhb = a1b * hb + a2b * hbs + xb
            nhbs = a1b * hbs - a2b * hb + xbs
            return nhf, nhfs, nhb, nhbs

        lax.fori_loop(0, n, step, (zero, zero, zero, zero))


def _chunk_permutation():
    L, p = SSM_CHUNK, SSM_GROUP
    gpb = LANES // p
    i = jnp.arange(L * LANES)
    dst = ((i % LANES) // p) * (L * p) + (i // LANES) * p + i % p
    return (dst[:, None] == jnp.arange(L * LANES)[None, :]).astype(BF16), gpb


def _s5_gather_body(u_ref, perm_ref, o_ref):
    ucat = jnp.concatenate([u_ref[:, l, :].astype(BF16) for l in range(u_ref.shape[1])], axis=1)
    res = jnp.dot(ucat, perm_ref[...], preferred_element_type=F32)
    lp = o_ref.shape[2]
    for g8 in range(o_ref.shape[0]):
        o_ref[g8] = res[:, g8 * lp:(g8 + 1) * lp].astype(o_ref.dtype)


def _s5y_body(u_ref, m_ref, hf_ref, hb_ref, cyf_ref, cyb_ref, permt_ref, y_ref):
    ys = []
    for g8 in range(u_ref.shape[0]):
        ls = slice(g8 * LANES, (g8 + 1) * LANES)
        y = jnp.dot(u_ref[g8], m_ref[g8], preferred_element_type=F32)
        y = y + jnp.dot(hf_ref[:, ls].astype(BF16), cyf_ref[g8], preferred_element_type=F32)
        y = y + jnp.dot(hb_ref[:, ls].astype(BF16), cyb_ref[g8], preferred_element_type=F32)
        ys.append(y)
    ycat = jnp.concatenate(ys, axis=1)
    hi = ycat.astype(BF16)
    lo = (ycat - hi.astype(F32)).astype(BF16)
    res = (jnp.dot(hi, permt_ref[...], preferred_element_type=F32)
           + jnp.dot(lo, permt_ref[...], preferred_element_type=F32))
    for l in range(y_ref.shape[1]):
        y_ref[:, l, :] = res[:, l * LANES:(l + 1) * LANES]


def s5_mix(zs, col0, weights, seqs):
    m, bx, cyf, cyb, a1f, a2f, a1b, a2b = weights
    t, ncols = zs.shape
    g = m.shape[0]
    L, p = SSM_CHUNK, SSM_GROUP
    lp = L * p
    c = t // L
    perm, gpb = _chunk_permutation()
    nblk = g // gpb
    tr = _tile(256, c)
    ug = pl.pallas_call(
        _s5_gather_body,
        grid=(nblk, c // tr),
        in_specs=[pl.BlockSpec((tr, L, LANES), lambda b, i: (i, 0, col0 // LANES + b)),
                  pl.BlockSpec((L * LANES, L * LANES), lambda b, i: (0, 0))],
        out_specs=pl.BlockSpec((gpb, tr, lp), lambda b, i: (b, i, 0)),
        out_shape=jax.ShapeDtypeStruct((g, c, lp), BF16),
        compiler_params=_cparams(("parallel", "parallel")),
        name="s5_gather_chunks",
    )(zs.reshape(c, L, ncols), perm)
    xs = pl.pallas_call(
        _s5x_body,
        grid=(g,),
        in_specs=[pl.BlockSpec((1, c, lp), lambda i: (i, 0, 0)),
                  pl.BlockSpec((1, lp, 4 * LANES), lambda i: (i, 0, 0))],
        out_specs=[pl.BlockSpec((c, LANES), lambda i: (0, i))] * 4,
        out_shape=[jax.ShapeDtypeStruct((c, g * LANES), F32)] * 4,
        compiler_params=_cparams(("parallel",)),
        name="s5_chunk_states",
    )(ug, bx)
    seq_chunks = tuple((a // L, (b - a) // L) for a, b in seqs)
    tc = _tile(512, g * LANES)
    xspec = pl.BlockSpec((c, tc), lambda i: (0, i))
    aspec = pl.BlockSpec((1, tc), lambda i: (0, i))
    hf, hb = pl.pallas_call(
        functools.partial(_s5scan_body, seq_chunks=seq_chunks),
        grid=(g * LANES // tc,),
        in_specs=[xspec] * 4 + [aspec] * 4,
        out_specs=[xspec] * 2,
        out_shape=[jax.ShapeDtypeStruct((c, g * LANES), F32)] * 2,
        compiler_params=_cparams(("parallel",)),
        name="s5_state_scan",
    )(*xs, a1f, a2f, a1b, a2b)
    y = pl.pallas_call(
        _s5y_body,
        grid=(nblk, c // tr),
        in_specs=[pl.BlockSpec((gpb, tr, lp), lambda b, i: (b, i, 0)),
                  pl.BlockSpec((gpb, lp, lp), lambda b, i: (b, 0, 0)),
                  pl.BlockSpec((tr, gpb * LANES), lambda b, i: (i, b)),
                  pl.BlockSpec((tr, gpb * LANES), lambda b, i: (i, b)),
                  pl.BlockSpec((gpb, LANES, lp), lambda b, i: (b, 0, 0)),
                  pl.BlockSpec((gpb, LANES, lp), lambda b, i: (b, 0, 0)),
                  pl.BlockSpec((L * LANES, L * LANES), lambda b, i: (0, 0))],
        out_specs=pl.BlockSpec((tr, L, LANES), lambda b, i: (i, 0, b)),
        out_shape=jax.ShapeDtypeStruct((c, L, g * p), F32),
        compiler_params=_cparams(("parallel", "parallel")),
        name="s5_outputs",
    )(ug, m, hf, hb, cyf, cyb, perm.T)
    return y.reshape(t, g * p)


def _glu_body(y_ref, w_ref, o_ref):
    gl = _gelu_tanh(y_ref[...])
    zz = jnp.dot(gl.astype(BF16), w_ref[...], preferred_element_type=F32)
    o_ref[...] = (gl * jax.nn.sigmoid(zz)).astype(o_ref.dtype)


def glu(y, w_glu, tm=512):
    t, w = y.shape
    tm = _tile(tm, t)
    return pl.pallas_call(
        _glu_body,
        grid=(t // tm,),
        in_specs=[pl.BlockSpec((tm, w), lambda i: (i, 0)),
                  pl.BlockSpec((w, w), lambda i: (0, 0))],
        out_specs=pl.BlockSpec((tm, w), lambda i: (i, 0)),
        out_shape=jax.ShapeDtypeStruct((t, w), BF16),
        compiler_params=_cparams(("parallel",)),
        name="s5_glu",
    )(y, w_glu)


def _merge_body(ya_ref, yb_ref, yc_ref, pa_ref, pb_ref, pc_ref, ga_ref, gb_ref, gc_ref, o_ref):
    a = jnp.dot(ya_ref[...], pa_ref[...], preferred_element_type=F32)
    b = jnp.dot(yb_ref[...], pb_ref[...], preferred_element_type=F32)
    c = jnp.dot(yc_ref[...], pc_ref[...], preferred_element_type=F32)
    o = ga_ref[...].astype(F32) * a + gb_ref[...].astype(F32) * b + gc_ref[...].astype(F32) * c
    o_ref[...] = o.astype(o_ref.dtype)


def gated_merge(ya, yb, yc, pa, pb, pc, gates, tm=512, tn=1024):
    t = ya.shape[0]
    d = pa.shape[1]
    tm = _tile(tm, t)
    tn = _tile(tn, d)
    nj = d // tn

    def aspec(x):
        return pl.BlockSpec((tm, x.shape[1]), lambda i, j: (i, 0))

    def wspec(x):
        return pl.BlockSpec((x.shape[0], tn), lambda i, j: (0, j))

    gspecs = [pl.BlockSpec((tm, tn), functools.partial(lambda i, j, k: (i, k * nj + j), k=k)) for k in range(3)]
    return pl.pallas_call(
        _merge_body,
        grid=(t // tm, nj),
        in_specs=[aspec(ya), aspec(yb), aspec(yc), wspec(pa), wspec(pb), wspec(pc)] + gspecs,
        out_specs=pl.BlockSpec((tm, tn), lambda i, j: (i, j)),
        out_shape=jax.ShapeDtypeStruct((t, d), BF16),
        compiler_params=_cparams(("parallel", "parallel")),
        name="gated_merge",
    )(ya, yb, yc, pa, pb, pc, gates, gates, gates)


def _oddeven_merge(lo, hi, r):
    step = r * 2
    if step < hi - lo:
        yield from _oddeven_merge(lo, hi, step)
        yield from _oddeven_merge(lo + r, hi, step)
        yield from [(i, i + r) for i in range(lo + r, hi - r, step)]
    else:
        yield (lo, lo + r)


def _oddeven_merge_sort(lo, hi):
    if hi - lo >= 1:
        mid = lo + (hi - lo) // 2
        yield from _oddeven_merge_sort(lo, mid)
        yield from _oddeven_merge_sort(mid + 1, hi)
        yield from _oddeven_merge(lo, hi, 1)


_SORT16 = tuple(_oddeven_merge_sort(0, PEER_TOPK - 1))
_BITONIC16 = tuple((i, i + d) for d in (8, 4, 2, 1) for i in range(PEER_TOPK) if not i & d)


def _compare_exchange(v, pairs):
    for i, j in pairs:
        hi = jnp.maximum(v[i], v[j])
        lo = jnp.minimum(v[i], v[j])
        v[i], v[j] = hi, lo


def _top16_desc(v):
    v = list(v)
    _compare_exchange(v, _SORT16)
    for shift in (4, 2, 1):
        r = [pltpu.roll(x, shift, axis=0) for x in v]
        v = [jnp.maximum(v[i], r[PEER_TOPK - 1 - i]) for i in range(PEER_TOPK)]
        _compare_exchange(v, _BITONIC16)
    return v


def _peer_select_body(q_ref, keys_ref, s1_ref, s2_ref, e2_ref, tau_ref, m1_ref, *, heads):
    tt = q_ref.shape[0]
    sub = lax.broadcasted_iota(jnp.int32, (SUBLANES, tt), 0)
    ninf = jnp.full((SUBLANES, tt), -jnp.inf, F32)

    def by_sublane(lst):
        out = lst[SUBLANES - 1]
        for j in range(SUBLANES - 2, -1, -1):
            out = jnp.where(sub == j, lst[j], out)
        return out

    for h in range(heads):
        scores, tops = [], []
        for c in range(2):
            blk = 2 * h + c
            qb = q_ref[:, blk * N_KEYS:(blk + 1) * N_KEYS]
            kb = keys_ref[blk]
            q_hi, k_hi = qb.astype(BF16), kb.astype(BF16)
            q_lo = (qb - q_hi.astype(F32)).astype(BF16)
            k_lo = (kb - k_hi.astype(F32)).astype(BF16)
            nt = (((1,), (1,)), ((), ()))
            st = (lax.dot_general(k_hi, q_hi, nt, preferred_element_type=F32)
                  + lax.dot_general(k_hi, q_lo, nt, preferred_element_type=F32)
                  + lax.dot_general(k_lo, q_hi, nt, preferred_element_type=F32))
            scores.append(st)
            tops.append(_top16_desc([st[SUBLANES * a:SUBLANES * (a + 1), :] for a in range(N_KEYS // SUBLANES)]))
        t1, t2 = tops
        t2a, t2b, t1b = by_sublane(t2[:8]), by_sublane(t2[8:]), by_sublane(t1[8:])
        cand = [t1[0] + t2a, t1[0] + t2b, t1[1] + t2a,
                jnp.where(sub < 5, t1[2] + t2a, ninf), jnp.where(sub < 4, t1[3] + t2a, ninf),
                jnp.where(sub < 3, t1[4] + t2a, ninf), jnp.where(sub < 2, t1[5] + t2a, ninf),
                jnp.where(sub < 2, t1[6] + t2a, ninf), jnp.where(sub < 2, t1[7] + t2a, ninf),
                t1b + t2[0]] + [ninf] * 6
        best = _top16_desc(cand)
        z = jnp.exp(best[0] - best[0])
        for b in best[1:]:
            z = z + jnp.exp(b - best[0])
        s1_ref[h] = scores[0]
        s2_ref[h] = scores[1]
        e2_ref[h] = jnp.exp(scores[1] - t2[0][0:1, :]) / z[0:1, :]
        tau_ref[h:h + 1, :] = best[PEER_TOPK - 1][0:1, :]
        m1_ref[h:h + 1, :] = t1[0][0:1, :]


def peer_select(q, keys, heads, tt=256):
    t = q.shape[0]
    tt = _tile(tt, t)
    big = jax.ShapeDtypeStruct((heads, N_KEYS, t), F32)
    small = jax.ShapeDtypeStruct((heads, t), F32)
    bspec = pl.BlockSpec((heads, N_KEYS, tt), lambda i: (0, 0, i))
    sspec = pl.BlockSpec((heads, tt), lambda i: (0, i))
    return pl.pallas_call(
        functools.partial(_peer_select_body, heads=heads),
        grid=(t // tt,),
        in_specs=[pl.BlockSpec((tt, q.shape[1]), lambda i: (i, 0)),
                  pl.BlockSpec(keys.shape, lambda i: (0, 0, 0))],
        out_specs=[bspec, bspec, bspec, sspec, sspec],
        out_shape=[big, big, big, small, small],
        compiler_params=_cparams(("parallel",)),
        name="peer_select",
    )(q, keys)


def _peer_dense_body(x_ref, h_ref, u_ref, v_ref, sc_ref, s1_ref, s2_ref, e2_ref, tau_ref, m1_ref, o_ref, *, heads):
    j = pl.program_id(1)
    te = u_ref.shape[0]
    na = te // N_KEYS

    @pl.when(j == 0)
    def _():
        o_ref[...] = x_ref[...]

    act = lax.dot_general(u_ref[...], h_ref[...], (((1,), (1,)), ((), ())), preferred_element_type=F32)
    act = act * sc_ref[:, 0:1]
    rows = []
    for al in range(na):
        a = j * na + al
        g = None
        for h in range(heads):
            s1row = s1_ref[h, pl.ds(a, 1), :]
            e1row = jnp.exp(s1row - m1_ref[h:h + 1, :]).astype(BF16)
            sel = (s1row + s2_ref[h]) >= tau_ref[h:h + 1, :]
            contrib = jnp.where(sel, e1row * e2_ref[h].astype(BF16), jnp.zeros((), BF16))
            g = contrib if g is None else g + contrib
        rows.append(g)
    gate = rows[0] if na == 1 else jnp.concatenate(rows, axis=0)
    wgt = gate.astype(F32) * _gelu_tanh(act)
    peak = jnp.max(jnp.abs(wgt), axis=(0, 1), keepdims=True)
    ws = jnp.where(peak > 0.0, jnp.exp2(jnp.floor(jnp.log2(F8_TARGET / jnp.where(peak > 0.0, peak, 1.0)))), 1.0)
    part = lax.dot_general((wgt * ws).astype(F8), v_ref[...], (((0,), (0,)), ((), ())),
                           preferred_element_type=F32)
    o_ref[...] += part * (sc_ref[:, 1:2] / ws)


def peer_dense(x, h, u_tab, v_tab, layer, scales, s1, s2, e2, tau, m1, heads, tm=512, te=1024):
    t, d = h.shape
    e = u_tab.shape[1]
    tm = _tile(tm, t)
    te = _tile(te, e)
    once = pl.Buffered(1)
    bspec = pl.BlockSpec((heads, N_KEYS, tm), lambda i, j: (0, 0, i), pipeline_mode=once)
    sspec = pl.BlockSpec((heads, tm), lambda i, j: (0, i))
    return pl.pallas_call(
        functools.partial(_peer_dense_body, heads=heads),
        grid=(t // tm, e // te),
        in_specs=[pl.BlockSpec((tm, d), lambda i, j: (i, 0), pipeline_mode=once),
                  pl.BlockSpec((tm, d), lambda i, j: (i, 0), pipeline_mode=once),
                  pl.BlockSpec((None, te, d), lambda i, j: (layer, j, 0)),
                  pl.BlockSpec((None, te, d), lambda i, j: (layer, j, 0)),
                  pl.BlockSpec((1, 2), lambda i, j: (0, 0)),
                  bspec, bspec, bspec, sspec, sspec],
        out_specs=pl.BlockSpec((tm, d), lambda i, j: (i, 0)),
        out_shape=jax.ShapeDtypeStruct((t, d), F32),
        compiler_params=_cparams(("parallel", "arbitrary")),
        name="peer_dense",
    )(x, h, u_tab, v_tab, scales.astype(F32), s1, s2, e2, tau, m1)


def _rope_table(seqs):
    inv = 1.0 / (ROPE_BASE ** (jnp.arange(0, QK_ROPE, 2, dtype=F32) / QK_ROPE))
    pos = jnp.concatenate([jnp.arange(b - a, dtype=F32) for a, b in seqs])
    ang = pos[:, None] * inv[None, :]
    c, s = jnp.cos(ang), jnp.sin(ang)
    return jnp.concatenate([c, c, s, s], axis=-1)


def _rotate_half_cols(w):
    half = w.shape[-1] // 2
    return jnp.concatenate([-w[..., half:], w[..., :half]], axis=-1)


def _mixer(x, seqs, cs, norm_mix, w_in, pool_w, pool_scale, q_norm, w_uq, kv_norm, w_ukv,
           a_re, a_im, log_dt, b_re, b_im, c_re, c_im, d_skip, w_glu, p_a, p_b, p_c, w_o):
    t, d = x.shape
    pool_width = pool_w.shape[0] * pool_w.shape[1]
    q_lora = w_uq.shape[0]
    kv_lora = w_ukv.shape[0]
    heads = w_uq.shape[1]
    ssm_width = d_skip.shape[0]
    o1 = pool_width
    o2 = o1 + q_lora
    o3 = o2 + kv_lora
    o4 = o3 + QK_ROPE
    o5 = o4 + ssm_width

    w_kpe = w_in[:, o3:o4]
    small_cols = [w_in[:, :o1], w_in[:, o1:o2], w_in[:, o4:o5], w_in[:, o2:o3], w_kpe, _rotate_half_cols(w_kpe)]
    n_small = o5 + QK_ROPE
    tn_small = 1280
    n_pad = -n_small % tn_small
    w_small = jnp.concatenate(small_cols + [jnp.zeros((d, n_pad), w_in.dtype)], axis=1).astype(BF16)
    hn, hn8, s_h = rmsnorm(x, norm_mix, BF16, with_f8=True)
    zs = matmul(hn, w_small, F32, tm=512, tn=tn_small, name="in_proj_small")
    wg8, s_w = _to_f8(w_in[:, o5:])
    gates = matmul(hn8, wg8, BF16, tm=1024, tn=1024, act="sigmoid", out_scale=1.0 / (s_h * s_w),
                   name="in_proj_gates")
    cq_block = o1 // q_lora
    ssm_block = o2 // ssm_width
    ckv_block = (o2 + ssm_width) // kv_lora
    kpe_block = (o2 + ssm_width + kv_lora) // LANES
    assert o1 % q_lora == 0 and o2 % ssm_width == 0 and (o2 + ssm_width) % kv_lora == 0

    ya = pool_mixer(zs, pool_w, pool_scale, seqs)

    wq_nope = w_uq[:, :, :QK_NOPE]
    wq_pe = w_uq[:, :, QK_NOPE:]
    wq = jnp.concatenate([wq_nope, wq_pe, _rotate_half_cols(wq_pe)], axis=-1).reshape(q_lora, heads * HEAD_PAD)
    qp = q_proj(zs, cq_block, q_norm, wq.astype(BF16), cs, heads)
    wk = w_ukv[:, :, :QK_NOPE].reshape(kv_lora, heads * QK_NOPE).astype(BF16)
    wvt = w_ukv[:, :, QK_NOPE:].reshape(kv_lora, heads * V_HEAD).T.astype(BF16)
    kp, v = kv_proj(zs, ckv_block, kpe_block, kv_norm, wk, wvt, cs, heads)
    groups = []
    for a, b in seqs:
        if groups and groups[-1][2] == b - a and groups[-1][0] + groups[-1][1] * groups[-1][2] == a:
            groups[-1][1] += 1
        else:
            groups.append([a, 1, b - a])
    outs = [attention(qp, kp, v, row0, nseq, seq, heads) for row0, nseq, seq in groups]
    yb = outs[0] if len(outs) == 1 else jnp.concatenate(outs, axis=0)

    s5w = _s5_weights(a_re, a_im, log_dt, b_re, b_im, c_re, c_im, d_skip)
    yc = glu(s5_mix(zs, ssm_block * ssm_width, s5w, seqs), w_glu.astype(BF16))

    merged = gated_merge(ya, yb, yc, p_a.astype(BF16), p_b.astype(BF16), p_c.astype(BF16), gates)
    return matmul(merged, w_o.astype(BF16), F32, tm=1024, tn=512, residual=x, name="out_proj")


def _peer(x, layer, norm_ffn, peer_wq, peer_keys, u_tabs, u_scales, v_tabs, v_scales):
    h2, h8, s_h = rmsnorm(x, norm_ffn, BF16, with_f8=True)
    pheads = peer_keys.shape[0]
    q = matmul(h2, peer_wq.astype(BF16), F32, tm=1024, tn=1024, name="peer_query")
    keys = peer_keys.reshape(pheads * 2, N_KEYS, peer_keys.shape[-1]).astype(F32)
    s1, s2, e2, tau, m1 = peer_select(q, keys, pheads)
    scales = jnp.stack([1.0 / (s_h * u_scales[layer]), 1.0 / v_scales[layer]]).reshape(1, 2)
    return peer_dense(x, h8, u_tabs, v_tabs, layer, scales, s1, s2, e2, tau, m1, pheads)


def kernel(x_prompt, x_sample, norm_mix, w_in, pool_w, pool_scale, q_norm, w_uq, kv_norm, w_ukv, ssm_a_re, ssm_a_im, ssm_log_dt, ssm_b_re, ssm_b_im, ssm_c_re, ssm_c_im, ssm_d, w_glu, p_a, p_b, p_c, w_o, norm_ffn, peer_wq, peer_keys, peer_u, peer_v, final_norm):
    d = x_prompt.shape[-1]
    seqs = []
    row = 0
    for xs in (x_prompt, x_sample):
        for _ in range(xs.shape[0]):
            seqs.append((row, row + xs.shape[1]))
            row += xs.shape[1]
    x = jnp.concatenate([x_prompt.reshape(-1, d), x_sample.reshape(-1, d)], axis=0)
    cs = _rope_table(seqs)
    per_layer = (norm_mix, w_in.astype(BF16), pool_w, pool_scale, q_norm, w_uq, kv_norm, w_ukv, ssm_a_re, ssm_a_im, ssm_log_dt,
                 ssm_b_re, ssm_b_im, ssm_c_re, ssm_c_im, ssm_d, w_glu, p_a, p_b, p_c, w_o, norm_ffn, peer_wq,
                 peer_keys)
    n_mix = 21
    u_scales = _pow2_scale(jnp.max(jnp.abs(peer_u), axis=(1, 2)))
    v_scales = _pow2_scale(jnp.max(jnp.abs(peer_v), axis=(1, 2)))
    u_tabs = (peer_u * u_scales[:, None, None]).astype(F8)
    v_tabs = (peer_v * v_scales[:, None, None]).astype(F8)
    for layer in range(norm_mix.shape[0]):
        lw = [w[layer] for w in per_layer]
        x = _mixer(x, seqs, cs, *lw[:n_mix])
        x = _peer(x, layer, *lw[n_mix:], u_tabs, u_scales, v_tabs, v_scales)
    n_prompt = x_prompt.shape[0] * x_prompt.shape[1]
    y_prompt = rmsnorm(x, final_norm, x_prompt.dtype, row0=0, nrows=n_prompt)
    y_sample = rmsnorm(x, final_norm, x_sample.dtype, row0=n_prompt, nrows=x.shape[0] - n_prompt)
    return (y_prompt.reshape(x_prompt.shape), y_sample.reshape(x_sample.shape))
```

```python
import functools
import math

import jax
import jax.numpy as jnp
from jax import lax
from jax.experimental import pallas as pl
from jax.experimental.pallas import tpu as pltpu

F32 = jnp.float32
BF16 = jnp.bfloat16
F8 = jnp.float8_e4m3fn
F8_TARGET = 240.0
F8_MAX_EXPO = 100.0

EPS = 1e-6
POOL_WINDOWS = (2, 4, 8, 16)
QK_NOPE = 128
QK_ROPE = 64
V_HEAD = 128
ROPE_BASE = 10000.0
SSM_GROUP = 16
SSM_STATE = 64
PEER_TOPK = 16
N_KEYS = 128

LANES = 128
SUBLANES = 8
HEAD_PAD = 256
SSM_CHUNK = 16
VMEM_LIMIT = 56 * 1024 * 1024
POOL_HALO = 8


def _cparams(sem, flags=None):
    return pltpu.CompilerParams(dimension_semantics=sem, vmem_limit_bytes=VMEM_LIMIT, flags=flags)


def _tile(pref, dim):
    t = min(pref, dim)
    assert dim % t == 0, (pref, dim)
    return t


def _gelu_tanh(x):
    cdf = 0.5 * (1.0 + jnp.tanh(math.sqrt(2.0 / math.pi) * (x + 0.044715 * (x * x * x))))
    return x * cdf


def _rms(x, g):
    y = x * lax.rsqrt(jnp.mean(x * x, axis=-1, keepdims=True) + EPS)
    return y * g


def _pow2_scale(bound):
    safe = jnp.where(bound > 0, bound, 1.0).astype(F32)
    expo = jnp.clip(jnp.floor(jnp.log2(F8_TARGET / safe)), -F8_MAX_EXPO, F8_MAX_EXPO)
    return jnp.where(bound > 0, jnp.exp2(expo), 1.0).astype(F32)


def _to_f8(w):
    s = _pow2_scale(jnp.max(jnp.abs(w)))
    return (w.astype(F32) * s).astype(F8), s


def _rmsnorm_body(x_ref, g_ref, o_ref):
    o_ref[...] = _rms(x_ref[...].astype(F32), g_ref[...]).astype(o_ref.dtype)


def _rmsnorm_f8_body(x_ref, g_ref, s_ref, o_ref, o8_ref):
    y = _rms(x_ref[...].astype(F32), g_ref[...])
    o_ref[...] = y.astype(o_ref.dtype)
    o8_ref[...] = (y * s_ref[...]).astype(o8_ref.dtype)


def rmsnorm(x, g, out_dtype, tm=512, row0=0, nrows=None, with_f8=False):
    d = x.shape[1]
    t = x.shape[0] if nrows is None else nrows
    tm = _tile(tm, math.gcd(t, row0) if row0 else t)
    b0 = row0 // tm
    xspec = pl.BlockSpec((tm, d), lambda i: (b0 + i, 0))
    gspec = pl.BlockSpec((1, d), lambda i: (0, 0))
    ospec = pl.BlockSpec((tm, d), lambda i: (i, 0))
    g2 = g.reshape(1, d).astype(F32)
    if not with_f8:
        return pl.pallas_call(
            _rmsnorm_body,
            grid=(t // tm,),
            in_specs=[xspec, gspec],
            out_specs=ospec,
            out_shape=jax.ShapeDtypeStruct((t, d), out_dtype),
            compiler_params=_cparams(("parallel",)),
            name="rmsnorm",
        )(x, g2)
    s = _pow2_scale(math.sqrt(d) * jnp.max(jnp.abs(g2)))
    y, y8 = pl.pallas_call(
        _rmsnorm_f8_body,
        grid=(t // tm,),
        in_specs=[xspec, gspec, pl.BlockSpec((1, 1), lambda i: (0, 0))],
        out_specs=[ospec, ospec],
        out_shape=[jax.ShapeDtypeStruct((t, d), out_dtype), jax.ShapeDtypeStruct((t, d), F8)],
        compiler_params=_cparams(("parallel",)),
        name="rmsnorm_f8",
    )(x, g2, s.reshape(1, 1))
    return y, y8, s


def _mm_body(a_ref, b_ref, o_ref, *, act):
    acc = jnp.dot(a_ref[...], b_ref[...], preferred_element_type=F32)
    if act == "sigmoid":
        acc = jax.nn.sigmoid(acc)
    o_ref[...] = acc.astype(o_ref.dtype)


def _mm_scaled_body(a_ref, b_ref, s_ref, o_ref, *, act):
    acc = jnp.dot(a_ref[...], b_ref[...], preferred_element_type=F32) * s_ref[...]
    if act == "sigmoid":
        acc = jax.nn.sigmoid(acc)
    o_ref[...] = acc.astype(o_ref.dtype)


def _mm_res_body(a_ref, b_ref, r_ref, o_ref):
    acc = jnp.dot(a_ref[...], b_ref[...], preferred_element_type=F32)
    o_ref[...] = (r_ref[...] + acc).astype(o_ref.dtype)


def matmul(a, b, out_dtype, tm, tn, act=None, residual=None, out_scale=None, name="matmul"):
    m, k = a.shape
    n = b.shape[1]
    tm = _tile(tm, m)
    tn = _tile(tn, n)
    in_specs = [pl.BlockSpec((tm, k), lambda i, j: (i, 0)),
                pl.BlockSpec((k, tn), lambda i, j: (0, j))]
    args = [a, b]
    if out_scale is not None:
        body = functools.partial(_mm_scaled_body, act=act)
        in_specs.append(pl.BlockSpec((1, 1), lambda i, j: (0, 0)))
        args.append(out_scale.reshape(1, 1).astype(F32))
    elif residual is None:
        body = functools.partial(_mm_body, act=act)
    else:
        body = _mm_res_body
        in_specs.append(pl.BlockSpec((tm, tn), lambda i, j: (i, j)))
        args.append(residual)
    return pl.pallas_call(
        body,
        grid=(m // tm, n // tn),
        in_specs=in_specs,
        out_specs=pl.BlockSpec((tm, tn), lambda i, j: (i, j)),
        out_shape=jax.ShapeDtypeStruct((m, n), out_dtype),
        compiler_params=_cparams(("parallel", "parallel")),
        name=name,
    )(*args)


def _seq_bounds(r0, seqs):
    s0 = jnp.int32(0)
    s1 = jnp.int32(0)
    for a, b in seqs:
        inside = jnp.logical_and(r0 >= a, r0 < b)
        s0 = jnp.where(inside, jnp.int32(a), s0)
        s1 = jnp.where(inside, jnp.int32(b), s1)
    return s0, s1


def _pool_body(prev_ref, cur_ref, next_ref, w_ref, sc_ref, o_ref, *, ts, seqs, group):
    r0 = pl.program_id(0) * ts
    s0, s1 = _seq_bounds(r0, seqs)
    cur = cur_ref[...].astype(F32)
    prev = jnp.where(r0 > s0, prev_ref[...].astype(F32), 0.0)
    nxt = jnp.where(r0 + ts < s1, next_ref[...].astype(F32), 0.0)
    ext = jnp.concatenate([prev, cur, nxt], axis=0).astype(BF16)
    te = ts + 2 * POOL_HALO
    d = lax.broadcasted_iota(jnp.int32, (ts, te), 1) - lax.broadcasted_iota(jnp.int32, (ts, te), 0)
    rows = r0 + lax.broadcasted_iota(jnp.int32, (ts, group), 0)
    for gi, w in enumerate(POOL_WINDOWS):
        left = w // 2
        right = w - 1 - left
        band = jnp.where(d >= POOL_HALO - left, jnp.where(d <= POOL_HALO + right, 1.0, 0.0), 0.0).astype(BF16)
        sl = slice(gi * group, (gi + 1) * group)
        tot = jnp.dot(band, ext[:, sl], preferred_element_type=F32)
        lo = jnp.maximum(rows - left, s0)
        hi = jnp.minimum(rows + right, s1 - 1)
        cnt = (hi - lo + 1).astype(F32)
        p = tot / cnt - cur[:, sl]
        y = jnp.dot(p.astype(BF16), w_ref[gi], preferred_element_type=F32) * sc_ref[:, sl]
        o_ref[:, sl] = y.astype(o_ref.dtype)


def pool_mixer(zs, pool_w, pool_scale, seqs, ts=512):
    t = zs.shape[0]
    ng, group, _ = pool_w.shape
    width = ng * group
    ts = _tile(ts, min(b - a for a, b in seqs))
    hb = ts // POOL_HALO
    nrow8 = t // POOL_HALO
    body = functools.partial(_pool_body, ts=ts, seqs=tuple(seqs), group=group)
    return pl.pallas_call(
        body,
        grid=(t // ts,),
        in_specs=[pl.BlockSpec((POOL_HALO, width), lambda i: (jnp.maximum(i * hb - 1, 0), 0)),
                  pl.BlockSpec((ts, width), lambda i: (i, 0)),
                  pl.BlockSpec((POOL_HALO, width), lambda i: (jnp.minimum((i + 1) * hb, nrow8 - 1), 0)),
                  pl.BlockSpec((ng, group, group), lambda i: (0, 0, 0)),
                  pl.BlockSpec((1, width), lambda i: (0, 0))],
        out_specs=pl.BlockSpec((ts, width), lambda i: (i, 0)),
        out_shape=jax.ShapeDtypeStruct((t, width), BF16),
        compiler_params=_cparams(("parallel",)),
        name="pool_mixer",
    )(zs, zs, zs, pool_w.astype(BF16), pool_scale.reshape(1, width).astype(F32))


def _rope_combine(x, cs):
    tmp = x * cs
    return tmp + pltpu.roll(tmp, LANES // 2, axis=1)


def _qproj_body(z_ref, g_ref, w_ref, cs_ref, o_ref, *, heads, scale):
    cn = _rms(z_ref[...].astype(F32), g_ref[...]).astype(BF16)
    acc = jnp.dot(cn, w_ref[...], preferred_element_type=F32)
    cs = cs_ref[...] * scale
    for h in range(heads):
        lo = acc[:, h * HEAD_PAD:h * HEAD_PAD + LANES] * scale
        hi = _rope_combine(acc[:, h * HEAD_PAD + LANES:(h + 1) * HEAD_PAD], cs)
        o_ref[:, h * HEAD_PAD:h * HEAD_PAD + LANES] = lo.astype(o_ref.dtype)
        o_ref[:, h * HEAD_PAD + LANES:(h + 1) * HEAD_PAD] = hi.astype(o_ref.dtype)


def q_proj(zs, col_block, q_norm, wq, cs, heads, tm=512):
    t = zs.shape[0]
    r = wq.shape[0]
    tm = _tile(tm, t)
    scale = (QK_NOPE + QK_ROPE) ** -0.5 * math.log2(math.e)
    body = functools.partial(_qproj_body, heads=heads, scale=scale)
    return pl.pallas_call(
        body,
        grid=(t // tm,),
        in_specs=[pl.BlockSpec((tm, r), lambda i: (i, col_block)),
                  pl.BlockSpec((1, r), lambda i: (0, 0)),
                  pl.BlockSpec((r, heads * HEAD_PAD), lambda i: (0, 0)),
                  pl.BlockSpec((tm, LANES), lambda i: (i, 0))],
        out_specs=pl.BlockSpec((tm, heads * HEAD_PAD), lambda i: (i, 0)),
        out_shape=jax.ShapeDtypeStruct((t, heads * HEAD_PAD), BF16),
        compiler_params=_cparams(("parallel",)),
        name="q_proj",
    )(zs, q_norm.reshape(1, r).astype(F32), wq, cs)


def _kvproj_body(z_ref, g_ref, kpe_ref, wk_ref, wvt_ref, cs_ref, k_ref, vt_ref, *, heads):
    cn = _rms(z_ref[...].astype(F32), g_ref[...]).astype(BF16)
    kn = jnp.dot(cn, wk_ref[...], preferred_element_type=F32)
    vt = lax.dot_general(wvt_ref[...], cn, (((1,), (1,)), ((), ())), preferred_element_type=F32)
    kr = _rope_combine(kpe_ref[...].astype(F32), cs_ref[...])
    lane = lax.broadcasted_iota(jnp.int32, kr.shape, 1)
    kr = jnp.where(lane < QK_ROPE, kr, 0.0).astype(k_ref.dtype)
    for h in range(heads):
        k_ref[:, h * HEAD_PAD:h * HEAD_PAD + LANES] = kn[:, h * LANES:(h + 1) * LANES].astype(k_ref.dtype)
        k_ref[:, h * HEAD_PAD + LANES:(h + 1) * HEAD_PAD] = kr
    vt_ref[...] = vt.astype(vt_ref.dtype)


def kv_proj(zs, ckv_block, kpe_block, kv_norm, wk, wvt, cs, heads, tm=512):
    t = zs.shape[0]
    r = wk.shape[0]
    tm = _tile(tm, t)
    body = functools.partial(_kvproj_body, heads=heads)
    return pl.pallas_call(
        body,
        grid=(t // tm,),
        in_specs=[pl.BlockSpec((tm, r), lambda i: (i, ckv_block)),
                  pl.BlockSpec((1, r), lambda i: (0, 0)),
                  pl.BlockSpec((tm, LANES), lambda i: (i, kpe_block)),
                  pl.BlockSpec((r, heads * QK_NOPE), lambda i: (0, 0)),
                  pl.BlockSpec((heads * V_HEAD, r), lambda i: (0, 0)),
                  pl.BlockSpec((tm, LANES), lambda i: (i, 0))],
        out_specs=[pl.BlockSpec((tm, heads * HEAD_PAD), lambda i: (i, 0)),
                   pl.BlockSpec((heads * V_HEAD, tm), lambda i: (0, i))],
        out_shape=[jax.ShapeDtypeStruct((t, heads * HEAD_PAD), BF16),
                   jax.ShapeDtypeStruct((heads * V_HEAD, t), BF16)],
        compiler_params=_cparams(("parallel",)),
        name="kv_proj",
    )(zs, kv_norm.reshape(1, r).astype(F32), zs, wk, wvt, cs)


def _sublane_allreduce(x, op):
    for shift in (4, 2, 1):
        x = op(x, pltpu.roll(x, shift, axis=0))
    return x


def _attn_body(q_ref, k_ref, vt_ref, o_ref, s_scr, smax_scr, *, tk, nsplit):
    tq = q_ref.shape[0]
    tw = tq // nsplit
    nk = k_ref.shape[0] // tk
    vr = V_HEAD // SUBLANES
    qt = q_ref[...].astype(F32).T.astype(BF16)
    qts = [qt[:, i * tw:(i + 1) * tw] for i in range(nsplit)]

    def produce(j, slot):
        ks = k_ref[pl.ds(pl.multiple_of(j * tk, tk), tk), :]
        for i, qh in enumerate(qts):
            s = jnp.dot(ks, qh, preferred_element_type=F32).reshape(tk // SUBLANES, SUBLANES, tw)
            s_scr[slot, i] = s
            smax_scr[slot, i] = jnp.max(s, axis=0)

    def consume(j, slot, state):
        vts = vt_ref[:, pl.ds(pl.multiple_of(j * tk, tk), tk)]
        new = []
        for i, (m, l, acc) in enumerate(state):
            mn = jnp.maximum(m, _sublane_allreduce(smax_scr[slot, i], jnp.maximum))
            p = jnp.exp2(s_scr[slot, i] - mn[None])
            alpha = jnp.exp2(m - mn)
            l = alpha * l + jnp.sum(p, axis=0)
            pv = jnp.dot(vts, p.reshape(tk, tw).astype(BF16), preferred_element_type=F32)
            acc = alpha[None] * acc + pv.reshape(vr, SUBLANES, tw)
            new.append((mn, l, acc))
        return tuple(new)

    def pair(jj, state):
        j = 2 * jj
        produce(j + 1, 1)
        state = consume(j, 0, state)
        produce(j + 2, 0)
        return consume(j + 1, 1, state)

    state = tuple((jnp.full((SUBLANES, tw), -jnp.inf, F32), jnp.zeros((SUBLANES, tw), F32),
                   jnp.zeros((vr, SUBLANES, tw), F32)) for _ in range(nsplit))
    produce(0, 0)
    if nk == 1:
        fin = consume(0, 0, state)
    else:
        assert nk % 2 == 0
        state = lax.fori_loop(0, nk // 2 - 1, pair, state)
        produce(nk - 1, 1)
        fin = consume(nk - 1, 1, consume(nk - 2, 0, state))
    for i, (_, l, acc) in enumerate(fin):
        lt = _sublane_allreduce(l, jnp.add)
        o = (acc / lt[None]).reshape(V_HEAD, tw)
        o_ref[i * tw:(i + 1) * tw, :] = o.T.astype(o_ref.dtype)


def attention(qp, kp, vt, row0, nseq, seq, heads, tq=1024, tk=1024, nsplit=2):
    tq = _tile(tq, seq)
    tk = _tile(tk, seq)
    nq = seq // tq
    q0 = row0 // tq
    b0 = row0 // seq
    assert row0 % seq == 0
    return pl.pallas_call(
        functools.partial(_attn_body, tk=tk, nsplit=nsplit),
        grid=(nseq, heads, nq),
        in_specs=[pl.BlockSpec((tq, HEAD_PAD), lambda b, h, i: (q0 + b * nq + i, h)),
                  pl.BlockSpec((seq, HEAD_PAD), lambda b, h, i: (b0 + b, h)),
                  pl.BlockSpec((V_HEAD, seq), lambda b, h, i: (h, b0 + b))],
        out_specs=pl.BlockSpec((tq, V_HEAD), lambda b, h, i: (b * nq + i, h)),
        out_shape=jax.ShapeDtypeStruct((nseq * seq, heads * V_HEAD), BF16),
        scratch_shapes=[pltpu.VMEM((2, nsplit, tk // SUBLANES, SUBLANES, tq // nsplit), F32),
                        pltpu.VMEM((2, nsplit, SUBLANES, tq // nsplit), F32)],
        compiler_params=_cparams(("parallel", "parallel", "parallel")),
        name="mla_attention",
    )(qp, kp, vt)


def _s5_weights(a_re, a_im, log_dt, b_re, b_im, c_re, c_im, d_skip):
    L = SSM_CHUNK
    lam = lax.complex(a_re.astype(F32), a_im.astype(F32))
    dt = jnp.exp(log_dt.astype(F32))[..., None]
    z = lam * dt
    lam_bar = jnp.exp(z)
    ks = jnp.arange(L + 1, dtype=F32)
    pw = jnp.exp(z[..., None] * ks.astype(jnp.complex64))
    b_bar = ((lam_bar - 1.0) / lam)[..., None] * lax.complex(b_re.astype(F32), b_im.astype(F32))
    c_mat = lax.complex(c_re.astype(F32), c_im.astype(F32))
    g, n, p = b_bar.shape[1:]
    kern = jnp.einsum("dgpn,dgnk,dgnq->dgkpq", c_mat, pw[..., :L], b_bar).real

    def causal_toeplitz(k):
        kpad = jnp.concatenate([jnp.zeros((g, L - 1, p, p), F32), k], axis=1)
        return jnp.stack([kpad[:, L - 1 - lp:2 * L - 1 - lp] for lp in range(L)], axis=1)

    diag = jnp.eye(L, dtype=F32)[None, :, :, None, None]
    eye = jnp.eye(p, dtype=F32) * d_skip.astype(F32).reshape(g, 1, 1, p, 1)
    m = causal_toeplitz(kern[0]) + causal_toeplitz(kern[1]).transpose(0, 2, 1, 3, 4) + diag * eye
    m = m.transpose(0, 1, 4, 2, 3).reshape(g, L * p, L * p)

    cf = jnp.einsum("gnl,gnq->glqn", pw[0][..., :L][..., ::-1], b_bar[0])
    cb = jnp.einsum("gnl,gnq->glqn", pw[1][..., :L], b_bar[1])
    cf = cf.reshape(g, L * p, n)
    cb = cb.reshape(g, L * p, n)
    bx = jnp.concatenate([cf.real, cf.imag, cf.imag, cf.real, cb.real, cb.imag, cb.imag, cb.real], axis=-1)

    wf = jnp.einsum("gpn,gnl->gnlp", c_mat[0], pw[0][..., 1:]).reshape(g, n, L * p)
    wb = jnp.einsum("gpn,gnl->gnlp", c_mat[1], pw[1][..., 1:][..., ::-1]).reshape(g, n, L * p)
    cyf = jnp.concatenate([wf.real, -wf.imag], axis=1)
    cyb = jnp.concatenate([wb.real, -wb.imag], axis=1)

    al = pw[..., L]

    def pack(x, y):
        return jnp.concatenate([x, y], axis=-1).reshape(1, g * 2 * n)

    a1f, a2f = pack(al[0].real, al[0].real), pack(-al[0].imag, al[0].imag)
    a1b, a2b = pack(al[1].real, al[1].real), pack(-al[1].imag, al[1].imag)
    return (m.astype(BF16), bx.astype(BF16), cyf.astype(BF16), cyb.astype(BF16),
            a1f.astype(F32), a2f.astype(F32), a1b.astype(F32), a2b.astype(F32))


def _s5x_body(u_ref, bx_ref, xf_ref, xfs_ref, xb_ref, xbs_ref):
    x = jnp.dot(u_ref[0], bx_ref[0], preferred_element_type=F32)
    xf_ref[...] = x[:, 0 * LANES:1 * LANES]
    xfs_ref[...] = x[:, 1 * LANES:2 * LANES]
    xb_ref[...] = x[:, 2 * LANES:3 * LANES]
    xbs_ref[...] = x[:, 3 * LANES:4 * LANES]


def _s5scan_body(xf_ref, xfs_ref, xb_ref, xbs_ref, a1f_ref, a2f_ref, a1b_ref, a2b_ref, hf_ref, hb_ref, *,
                 seq_chunks):
    a1f, a2f, a1b, a2b = a1f_ref[...], a2f_ref[...], a1b_ref[...], a2b_ref[...]
    zero = jnp.zeros_like(a1f)
    for c0, n in seq_chunks:

        def step(i, carry, c0=c0, n=n):
            hf, hfs, hb, hbs = carry
            rf = c0 + i
            rb = c0 + n - 1 - i
            hf_ref[pl.ds(rf, 1), :] = hf
            hb_ref[pl.ds(rb, 1), :] = hb
            xf = xf_ref[pl.ds(rf, 1), :]
            xfs = xfs_ref[pl.ds(rf, 1), :]
            xb = xb_ref[pl.ds(rb, 1), :]
            xbs = xbs_ref[pl.ds(rb, 1), :]
            nhf = a1f * hf + a2f * hfs + xf
            nhfs = a1f * hfs - a2f * hf + xfs
            nhb = a1b * hb + a2b * hbs + xb
            nhbs = a1b * hbs - a2b * hb + xbs
            return nhf, nhfs, nhb, nhbs

        lax.fori_loop(0, n, step, (zero, zero, zero, zero))


def _chunk_permutation():
    L, p = SSM_CHUNK, SSM_GROUP
    gpb = LANES // p
    i = jnp.arange(L * LANES)
    dst = ((i % LANES) // p) * (L * p) + (i // LANES) * p + i % p
    return (dst[:, None] == jnp.arange(L * LANES)[None, :]).astype(BF16), gpb


def _s5_gather_body(u_ref, perm_ref, o_ref):
    ucat = jnp.concatenate([u_ref[:, l, :].astype(BF16) for l in range(u_ref.shape[1])], axis=1)
    res = jnp.dot(ucat, perm_ref[...], preferred_element_type=F32)
    lp = o_ref.shape[2]
    for g8 in range(o_ref.shape[0]):
        o_ref[g8] = res[:, g8 * lp:(g8 + 1) * lp].astype(o_ref.dtype)


def _s5y_body(u_ref, m_ref, hf_ref, hb_ref, cyf_ref, cyb_ref, permt_ref, y_ref):
    ys = []
    for g8 in range(u_ref.shape[0]):
        ls = slice(g8 * LANES, (g8 + 1) * LANES)
        y = jnp.dot(u_ref[g8], m_ref[g8], preferred_element_type=F32)
        y = y + jnp.dot(hf_ref[:, ls].astype(BF16), cyf_ref[g8], preferred_element_type=F32)
        y = y + jnp.dot(hb_ref[:, ls].astype(BF16), cyb_ref[g8], preferred_element_type=F32)
        ys.append(y)
    ycat = jnp.concatenate(ys, axis=1)
    hi = ycat.astype(BF16)
    lo = (ycat - hi.astype(F32)).astype(BF16)
    res = (jnp.dot(hi, permt_ref[...], preferred_element_type=F32)
           + jnp.dot(lo, permt_ref[...], preferred_element_type=F32))
    for l in range(y_ref.shape[1]):
        y_ref[:, l, :] = res[:, l * LANES:(l + 1) * LANES]


def s5_mix(zs, col0, weights, seqs):
    m, bx, cyf, cyb, a1f, a2f, a1b, a2b = weights
    t, ncols = zs.shape
    g = m.shape[0]
    L, p = SSM_CHUNK, SSM_GROUP
    lp = L * p
    c = t // L
    perm, gpb = _chunk_permutation()
    nblk = g // gpb
    tr = _tile(256, c)
    ug = pl.pallas_call(
        _s5_gather_body,
        grid=(nblk, c // tr),
        in_specs=[pl.BlockSpec((tr, L, LANES), lambda b, i: (i, 0, col0 // LANES + b)),
                  pl.BlockSpec((L * LANES, L * LANES), lambda b, i: (0, 0))],
        out_specs=pl.BlockSpec((gpb, tr, lp), lambda b, i: (b, i, 0)),
        out_shape=jax.ShapeDtypeStruct((g, c, lp), BF16),
        compiler_params=_cparams(("parallel", "parallel")),
        name="s5_gather_chunks",
    )(zs.reshape(c, L, ncols), perm)
    xs = pl.pallas_call(
        _s5x_body,
        grid=(g,),
        in_specs=[pl.BlockSpec((1, c, lp), lambda i: (i, 0, 0)),
                  pl.BlockSpec((1, lp, 4 * LANES), lambda i: (i, 0, 0))],
        out_specs=[pl.BlockSpec((c, LANES), lambda i: (0, i))] * 4,
        out_shape=[jax.ShapeDtypeStruct((c, g * LANES), F32)] * 4,
        compiler_params=_cparams(("parallel",)),
        name="s5_chunk_states",
    )(ug, bx)
    seq_chunks = tuple((a // L, (b - a) // L) for a, b in seqs)
    tc = _tile(512, g * LANES)
    xspec = pl.BlockSpec((c, tc), lambda i: (0, i))
    aspec = pl.BlockSpec((1, tc), lambda i: (0, i))
    hf, hb = pl.pallas_call(
        functools.partial(_s5scan_body, seq_chunks=seq_chunks),
        grid=(g * LANES // tc,),
        in_specs=[xspec] * 4 + [aspec] * 4,
        out_specs=[xspec] * 2,
        out_shape=[jax.ShapeDtypeStruct((c, g * LANES), F32)] * 2,
        compiler_params=_cparams(("parallel",)),
        name="s5_state_scan",
    )(*xs, a1f, a2f, a1b, a2b)
    y = pl.pallas_call(
        _s5y_body,
        grid=(nblk, c // tr),
        in_specs=[pl.BlockSpec((gpb, tr, lp), lambda b, i: (b, i, 0)),
                  pl.BlockSpec((gpb, lp, lp), lambda b, i: (b, 0, 0)),
                  pl.BlockSpec((tr, gpb * LANES), lambda b, i: (i, b)),
                  pl.BlockSpec((tr, gpb * LANES), lambda b, i: (i, b)),
                  pl.BlockSpec((gpb, LANES, lp), lambda b, i: (b, 0, 0)),
                  pl.BlockSpec((gpb, LANES, lp), lambda b, i: (b, 0, 0)),
                  pl.BlockSpec((L * LANES, L * LANES), lambda b, i: (0, 0))],
        out_specs=pl.BlockSpec((tr, L, LANES), lambda b, i: (i, 0, b)),
        out_shape=jax.ShapeDtypeStruct((c, L, g * p), F32),
        compiler_params=_cparams(("parallel", "parallel")),
        name="s5_outputs",
    )(ug, m, hf, hb, cyf, cyb, perm.T)
    return y.reshape(t, g * p)


def _glu_body(y_ref, w_ref, o_ref):
    gl = _gelu_tanh(y_ref[...])
    zz = jnp.dot(gl.astype(BF16), w_ref[...], preferred_element_type=F32)
    o_ref[...] = (gl * jax.nn.sigmoid(zz)).astype(o_ref.dtype)


def glu(y, w_glu, tm=512):
    t, w = y.shape
    tm = _tile(tm, t)
    return pl.pallas_call(
        _glu_body,
        grid=(t // tm,),
        in_specs=[pl.BlockSpec((tm, w), lambda i: (i, 0)),
                  pl.BlockSpec((w, w), lambda i: (0, 0))],
        out_specs=pl.BlockSpec((tm, w), lambda i: (i, 0)),
        out_shape=jax.ShapeDtypeStruct((t, w), BF16),
        compiler_params=_cparams(("parallel",)),
        name="s5_glu",
    )(y, w_glu)


def _merge_body(ya_ref, yb_ref, yc_ref, pa_ref, pb_ref, pc_ref, ga_ref, gb_ref, gc_ref, o_ref):
    a = jnp.dot(ya_ref[...], pa_ref[...], preferred_element_type=F32)
    b = jnp.dot(yb_ref[...], pb_ref[...], preferred_element_type=F32)
    c = jnp.dot(yc_ref[...], pc_ref[...], preferred_element_type=F32)
    o = ga_ref[...].astype(F32) * a + gb_ref[...].astype(F32) * b + gc_ref[...].astype(F32) * c
    o_ref[...] = o.astype(o_ref.dtype)


def gated_merge(ya, yb, yc, pa, pb, pc, gates, tm=512, tn=1024):
    t = ya.shape[0]
    d = pa.shape[1]
    tm = _tile(tm, t)
    tn = _tile(tn, d)
    nj = d // tn

    def aspec(x):
        return pl.BlockSpec((tm, x.shape[1]), lambda i, j: (i, 0))

    def wspec(x):
        return pl.BlockSpec((x.shape[0], tn), lambda i, j: (0, j))

    gspecs = [pl.BlockSpec((tm, tn), functools.partial(lambda i, j, k: (i, k * nj + j), k=k)) for k in range(3)]
    return pl.pallas_call(
        _merge_body,
        grid=(t // tm, nj),
        in_specs=[aspec(ya), aspec(yb), aspec(yc), wspec(pa), wspec(pb), wspec(pc)] + gspecs,
        out_specs=pl.BlockSpec((tm, tn), lambda i, j: (i, j)),
        out_shape=jax.ShapeDtypeStruct((t, d), BF16),
        compiler_params=_cparams(("parallel", "parallel")),
        name="gated_merge",
    )(ya, yb, yc, pa, pb, pc, gates, gates, gates)


def _oddeven_merge(lo, hi, r):
    step = r * 2
    if step < hi - lo:
        yield from _oddeven_merge(lo, hi, step)
        yield from _oddeven_merge(lo + r, hi, step)
        yield from [(i, i + r) for i in range(lo + r, hi - r, step)]
    else:
        yield (lo, lo + r)


def _oddeven_merge_sort(lo, hi):
    if hi - lo >= 1:
        mid = lo + (hi - lo) // 2
        yield from _oddeven_merge_sort(lo, mid)
        yield from _oddeven_merge_sort(mid + 1, hi)
        yield from _oddeven_merge(lo, hi, 1)


_SORT16 = tuple(_oddeven_merge_sort(0, PEER_TOPK - 1))
_BITONIC16 = tuple((i, i + d) for d in (8, 4, 2, 1) for i in range(PEER_TOPK) if not i & d)


def _compare_exchange(v, pairs):
    for i, j in pairs:
        hi = jnp.maximum(v[i], v[j])
        lo = jnp.minimum(v[i], v[j])
        v[i], v[j] = hi, lo


def _top16_desc(v):
    v = list(v)
    _compare_exchange(v, _SORT16)
    for shift in (4, 2, 1):
        r = [pltpu.roll(x, shift, axis=0) for x in v]
        v = [jnp.maximum(v[i], r[PEER_TOPK - 1 - i]) for i in range(PEER_TOPK)]
        _compare_exchange(v, _BITONIC16)
    return v


def _peer_select_body(q_ref, keys_ref, s1_ref, s2_ref, e2_ref, tau_ref, m1_ref, *, heads):
    tt = q_ref.shape[0]
    sub = lax.broadcasted_iota(jnp.int32, (SUBLANES, tt), 0)
    ninf = jnp.full((SUBLANES, tt), -jnp.inf, F32)

    def by_sublane(lst):
        out = lst[SUBLANES - 1]
        for j in range(SUBLANES - 2, -1, -1):
            out = jnp.where(sub == j, lst[j], out)
        return out

    for h in range(heads):
        scores, tops = [], []
        for c in range(2):
            blk = 2 * h + c
            qb = q_ref[:, blk * N_KEYS:(blk + 1) * N_KEYS]
            kb = keys_ref[blk]
            q_hi, k_hi = qb.astype(BF16), kb.astype(BF16)
            q_lo = (qb - q_hi.astype(F32)).astype(BF16)
            k_lo = (kb - k_hi.astype(F32)).astype(BF16)
            nt = (((1,), (1,)), ((), ()))
            st = (lax.dot_general(k_hi, q_hi, nt, preferred_element_type=F32)
                  + lax.dot_general(k_hi, q_lo, nt, preferred_element_type=F32)
                  + lax.dot_general(k_lo, q_hi, nt, preferred_element_type=F32))
            scores.append(st)
            tops.append(_top16_desc([st[SUBLANES * a:SUBLANES * (a + 1), :] for a in range(N_KEYS // SUBLANES)]))
        t1, t2 = tops
        t2a, t2b, t1b = by_sublane(t2[:8]), by_sublane(t2[8:]), by_sublane(t1[8:])
        cand = [t1[0] + t2a, t1[0] + t2b, t1[1] + t2a,
                jnp.where(sub < 5, t1[2] + t2a, ninf), jnp.where(sub < 4, t1[3] + t2a, ninf),
                jnp.where(sub < 3, t1[4] + t2a, ninf), jnp.where(sub < 2, t1[5] + t2a, ninf),
                jnp.where(sub < 2, t1[6] + t2a, ninf), jnp.where(sub < 2, t1[7] + t2a, ninf),
                t1b + t2[0]] + [ninf] * 6
        best = _top16_desc(cand)
        z = jnp.exp(best[0] - best[0])
        for b in best[1:]:
            z = z + jnp.exp(b - best[0])
        s1_ref[h] = scores[0]
        s2_ref[h] = scores[1]
        e2_ref[h] = jnp.exp(scores[1] - t2[0][0:1, :]) / z[0:1, :]
        tau_ref[h:h + 1, :] = best[PEER_TOPK - 1][0:1, :]
        m1_ref[h:h + 1, :] = t1[0][0:1, :]


def peer_select(q, keys, heads, tt=256):
    t = q.shape[0]
    tt = _tile(tt, t)
    big = jax.ShapeDtypeStruct((heads, N_KEYS, t), F32)
    small = jax.ShapeDtypeStruct((heads, t), F32)
    bspec = pl.BlockSpec((heads, N_KEYS, tt), lambda i: (0, 0, i))
    sspec = pl.BlockSpec((heads, tt), lambda i: (0, i))
    return pl.pallas_call(
        functools.partial(_peer_select_body, heads=heads),
        grid=(t // tt,),
        in_specs=[pl.BlockSpec((tt, q.shape[1]), lambda i: (i, 0)),
                  pl.BlockSpec(keys.shape, lambda i: (0, 0, 0))],
        out_specs=[bspec, bspec, bspec, sspec, sspec],
        out_shape=[big, big, big, small, small],
        compiler_params=_cparams(("parallel",)),
        name="peer_select",
    )(q, keys)


def _peer_dense_body(x_ref, h_ref, u_ref, v_ref, sc_ref, s1_ref, s2_ref, e2_ref, tau_ref, m1_ref, o_ref, *, heads):
    j = pl.program_id(1)
    te = u_ref.shape[0]
    na = te // N_KEYS

    @pl.when(j == 0)
    def _():
        o_ref[...] = x_ref[...]

    act = lax.dot_general(u_ref[...], h_ref[...], (((1,), (1,)), ((), ())), preferred_element_type=F32)
    act = act * sc_ref[:, 0:1]
    rows = []
    for al in range(na):
        a = j * na + al
        g = None
        for h in range(heads):
            s1row = s1_ref[h, pl.ds(a, 1), :]
            e1row = jnp.exp(s1row - m1_ref[h:h + 1, :]).astype(BF16)
            sel = (s1row + s2_ref[h]) >= tau_ref[h:h + 1, :]
            contrib = jnp.where(sel, e1row * e2_ref[h].astype(BF16), jnp.zeros((), BF16))
            g = contrib if g is None else g + contrib
        rows.append(g)
    gate = rows[0] if na == 1 else jnp.concatenate(rows, axis=0)
    wgt = gate.astype(F32) * _gelu_tanh(act)
    peak = jnp.max(jnp.abs(wgt), axis=(0, 1), keepdims=True)
    expo = jnp.floor(jnp.log2(F8_TARGET / jnp.where(peak > 0.0, peak, 1.0)))
    ws = jnp.where(peak > 0.0, jnp.exp2(jnp.clip(expo, -F8_MAX_EXPO, F8_MAX_EXPO)), 1.0)
    part = lax.dot_general((wgt * ws).astype(F8), v_ref[...], (((0,), (0,)), ((), ())),
                           preferred_element_type=F32)
    o_ref[...] += part * (sc_ref[:, 1:2] / ws)


def peer_dense(x, h, u_tab, v_tab, layer, scales, s1, s2, e2, tau, m1, heads, tm=512, te=1024):
    t, d = h.shape
    e = u_tab.shape[1]
    tm = _tile(tm, t)
    te = _tile(te, e)
    once = pl.Buffered(1)
    bspec = pl.BlockSpec((heads, N_KEYS, tm), lambda i, j: (0, 0, i), pipeline_mode=once)
    sspec = pl.BlockSpec((heads, tm), lambda i, j: (0, i))
    return pl.pallas_call(
        functools.partial(_peer_dense_body, heads=heads),
        grid=(t // tm, e // te),
        in_specs=[pl.BlockSpec((tm, d), lambda i, j: (i, 0), pipeline_mode=once),
                  pl.BlockSpec((tm, d), lambda i, j: (i, 0), pipeline_mode=once),
                  pl.BlockSpec((None, te, d), lambda i, j: (layer, j, 0)),
                  pl.BlockSpec((None, te, d), lambda i, j: (layer, j, 0)),
                  pl.BlockSpec((1, 2), lambda i, j: (0, 0)),
                  bspec, bspec, bspec, sspec, sspec],
        out_specs=pl.BlockSpec((tm, d), lambda i, j: (i, 0)),
        out_shape=jax.ShapeDtypeStruct((t, d), F32),
        compiler_params=_cparams(("parallel", "arbitrary")),
        name="peer_dense",
    )(x, h, u_tab, v_tab, scales.astype(F32), s1, s2, e2, tau, m1)


def _rope_table(seqs):
    inv = 1.0 / (ROPE_BASE ** (jnp.arange(0, QK_ROPE, 2, dtype=F32) / QK_ROPE))
    pos = jnp.concatenate([jnp.arange(b - a, dtype=F32) for a, b in seqs])
    ang = pos[:, None] * inv[None, :]
    c, s = jnp.cos(ang), jnp.sin(ang)
    return jnp.concatenate([c, c, s, s], axis=-1)


def _rotate_half_cols(w):
    half = w.shape[-1] // 2
    return jnp.concatenate([-w[..., half:], w[..., :half]], axis=-1)


def _mixer(x, seqs, cs, norm_mix, w_in, pool_w, pool_scale, q_norm, w_uq, kv_norm, w_ukv,
           a_re, a_im, log_dt, b_re, b_im, c_re, c_im, d_skip, w_glu, p_a, p_b, p_c, w_o):
    t, d = x.shape
    pool_width = pool_w.shape[0] * pool_w.shape[1]
    q_lora = w_uq.shape[0]
    kv_lora = w_ukv.shape[0]
    heads = w_uq.shape[1]
    ssm_width = d_skip.shape[0]
    o1 = pool_width
    o2 = o1 + q_lora
    o3 = o2 + kv_lora
    o4 = o3 + QK_ROPE
    o5 = o4 + ssm_width

    w_kpe = w_in[:, o3:o4]
    small_cols = [w_in[:, :o1], w_in[:, o1:o2], w_in[:, o4:o5], w_in[:, o2:o3], w_kpe, _rotate_half_cols(w_kpe)]
    n_small = o5 + QK_ROPE
    tn_small = 1280
    n_pad = -n_small % tn_small
    w_small = jnp.concatenate(small_cols + [jnp.zeros((d, n_pad), w_in.dtype)], axis=1).astype(BF16)
    hn, hn8, s_h = rmsnorm(x, norm_mix, BF16, with_f8=True)
    zs = matmul(hn, w_small, F32, tm=512, tn=tn_small, name="in_proj_small")
    wg8, s_w = _to_f8(w_in[:, o5:])
    gates = matmul(hn8, wg8, BF16, tm=1024, tn=1024, act="sigmoid", out_scale=1.0 / (s_h * s_w),
                   name="in_proj_gates")
    cq_block = o1 // q_lora
    ssm_block = o2 // ssm_width
    ckv_block = (o2 + ssm_width) // kv_lora
    kpe_block = (o2 + ssm_width + kv_lora) // LANES
    assert o1 % q_lora == 0 and o2 % ssm_width == 0 and (o2 + ssm_width) % kv_lora == 0

    ya = pool_mixer(zs, pool_w, pool_scale, seqs)

    wq_nope = w_uq[:, :, :QK_NOPE]
    wq_pe = w_uq[:, :, QK_NOPE:]
    wq = jnp.concatenate([wq_nope, wq_pe, _rotate_half_cols(wq_pe)], axis=-1).reshape(q_lora, heads * HEAD_PAD)
    qp = q_proj(zs, cq_block, q_norm, wq.astype(BF16), cs, heads)
    wk = w_ukv[:, :, :QK_NOPE].reshape(kv_lora, heads * QK_NOPE).astype(BF16)
    wvt = w_ukv[:, :, QK_NOPE:].reshape(kv_lora, heads * V_HEAD).T.astype(BF16)
    kp, v = kv_proj(zs, ckv_block, kpe_block, kv_norm, wk, wvt, cs, heads)
    groups = []
    for a, b in seqs:
        if groups and groups[-1][2] == b - a and groups[-1][0] + groups[-1][1] * groups[-1][2] == a:
            groups[-1][1] += 1
        else:
            groups.append([a, 1, b - a])
    outs = [attention(qp, kp, v, row0, nseq, seq, heads) for row0, nseq, seq in groups]
    yb = outs[0] if len(outs) == 1 else jnp.concatenate(outs, axis=0)

    s5w = _s5_weights(a_re, a_im, log_dt, b_re, b_im, c_re, c_im, d_skip)
    yc = glu(s5_mix(zs, ssm_block * ssm_width, s5w, seqs), w_glu.astype(BF16))

    merged = gated_merge(ya, yb, yc, p_a.astype(BF16), p_b.astype(BF16), p_c.astype(BF16), gates)
    return matmul(merged, w_o.astype(BF16), F32, tm=1024, tn=512, residual=x, name="out_proj")


def _peer(x, layer, norm_ffn, peer_wq, peer_keys, u_tabs, u_scales, v_tabs, v_scales):
    h2, h8, s_h = rmsnorm(x, norm_ffn, BF16, with_f8=True)
    pheads = peer_keys.shape[0]
    q = matmul(h2, peer_wq.astype(BF16), F32, tm=1024, tn=1024, name="peer_query")
    keys = peer_keys.reshape(pheads * 2, N_KEYS, peer_keys.shape[-1]).astype(F32)
    s1, s2, e2, tau, m1 = peer_select(q, keys, pheads)
    scales = jnp.stack([1.0 / (s_h * u_scales[layer]), 1.0 / v_scales[layer]]).reshape(1, 2)
    return peer_dense(x, h8, u_tabs, v_tabs, layer, scales, s1, s2, e2, tau, m1, pheads)


def kernel(x_prompt, x_sample, norm_mix, w_in, pool_w, pool_scale, q_norm, w_uq, kv_norm, w_ukv, ssm_a_re, ssm_a_im, ssm_log_dt, ssm_b_re, ssm_b_im, ssm_c_re, ssm_c_im, ssm_d, w_glu, p_a, p_b, p_c, w_o, norm_ffn, peer_wq, peer_keys, peer_u, peer_v, final_norm):
    d = x_prompt.shape[-1]
    seqs = []
    row = 0
    for xs in (x_prompt, x_sample):
        for _ in range(xs.shape[0]):
            seqs.append((row, row + xs.shape[1]))
            row += xs.shape[1]
    x = jnp.concatenate([x_prompt.reshape(-1, d), x_sample.reshape(-1, d)], axis=0)
    cs = _rope_table(seqs)
    per_layer = (norm_mix, w_in.astype(BF16), pool_w, pool_scale, q_norm, w_uq, kv_norm, w_ukv, ssm_a_re, ssm_a_im, ssm_log_dt,
                 ssm_b_re, ssm_b_im, ssm_c_re, ssm_c_im, ssm_d, w_glu, p_a, p_b, p_c, w_o, norm_ffn, peer_wq,
                 peer_keys)
    n_mix = 21
    u_scales = _pow2_scale(jnp.max(jnp.abs(peer_u), axis=(1, 2)))
    v_scales = _pow2_scale(jnp.max(jnp.abs(peer_v), axis=(1, 2)))
    u_tabs = (peer_u * u_scales[:, None, None]).astype(F8)
    v_tabs = (peer_v * v_scales[:, None, None]).astype(F8)
    for layer in range(norm_mix.shape[0]):
        lw = [w[layer] for w in per_layer]
        x = _mixer(x, seqs, cs, *lw[:n_mix])
        x = _peer(x, layer, *lw[n_mix:], u_tabs, u_scales, v_tabs, v_scales)
    n_prompt = x_prompt.shape[0] * x_prompt.shape[1]
    y_prompt = rmsnorm(x, final_norm, x_prompt.dtype, row0=0, nrows=n_prompt)
    y_sample = rmsnorm(x, final_norm, x_sample.dtype, row0=n_prompt, nrows=x.shape[0] - n_prompt)
    return (y_prompt.reshape(x_prompt.shape), y_sample.reshape(x_sample.shape))
```
